```python
import jax, jax.numpy as jnp
from jax import lax
import numpy as np

D_MODEL = 1024
BATCH = 4
SEQ = 4096
DEPTH = 4
DEC_BATCH = 128
DEC_SEQ = 4
PAST_LEN = 8192
PAGE_SIZE = 128

N_MIXERS = 3
N_CONV_LAYERS = (DEPTH + 2) // 3
N_MLA_LAYERS = (DEPTH + 1) // 3
N_RET_LAYERS = DEPTH // 3

CONV_WIDTH = 31
MLA_HEADS = 8
MLA_Q_LORA = 384
MLA_KV_LORA = 256
MLA_NOPE = 128
MLA_ROPE = 64
MLA_V = 128
MLA_Q_BLOCK = 128
MLA_SCALE = (MLA_NOPE + MLA_ROPE) ** -0.5
ROPE_BASE = 10000.0
RET_HEADS = D_MODEL // 256
RET_DK = D_MODEL // RET_HEADS
RET_DV = 2 * RET_DK
RET_CHUNK = 128
D_FF = -(-8 * D_MODEL // (3 * 256)) * 256
EPS = 1e-6
NEG_INF = -1e30

kernel_name = "hybrid_conv_mla_retention_decode_step"


def _rmsnorm(x, g):
    xf = x.astype(jnp.float32)
    xf = xf * lax.rsqrt(jnp.mean(xf * xf, axis=-1, keepdims=True) + EPS)
    return (xf * g.astype(jnp.float32)).astype(x.dtype)


def _layernorm(x, g, b):
    xf = x.astype(jnp.float32)
    mu = jnp.mean(xf, axis=-1, keepdims=True)
    xc = xf - mu
    xf = xc * lax.rsqrt(jnp.mean(xc * xc, axis=-1, keepdims=True) + EPS)
    return (xf * g.astype(jnp.float32) + b.astype(jnp.float32)).astype(x.dtype)


def _rope(x, pos, inv_freq):
    half = x.shape[-1] // 2
    ang = pos.astype(jnp.float32)[:, None] * inv_freq[None, :]
    cos = jnp.cos(ang)[None, :, None, :].astype(x.dtype)
    sin = jnp.sin(ang)[None, :, None, :].astype(x.dtype)
    x1, x2 = x[..., :half], x[..., half:]
    return jnp.concatenate([x1 * cos - x2 * sin, x2 * cos + x1 * sin], axis=-1)


def _mla_inv_freq():
    return ROPE_BASE ** (-jnp.arange(0, MLA_ROPE, 2, dtype=jnp.float32) / MLA_ROPE)


def _ret_inv_freq():
    return 1.0 / (ROPE_BASE ** jnp.linspace(0.0, 1.0, RET_DK // 2, dtype=jnp.float32))


def _swiglu(h, w1, w3, w2):
    return (jax.nn.silu(h @ w1) * (h @ w3)) @ w2


def _conv_module(h, buf_prev, w_pw1, b_pw1, w_dw, b_dw, ln_g, ln_b, w_pw2, b_pw2):
    a = h @ w_pw1 + b_pw1
    u = a[..., :D_MODEL] * jax.nn.sigmoid(a[..., D_MODEL:])
    buf = jnp.concatenate([buf_prev.astype(u.dtype), u], axis=1)
    c = lax.conv_general_dilated(buf, w_dw[:, None, :].astype(buf.dtype), (1,), 'VALID',
                                 dimension_numbers=('NWC', 'WIO', 'NWC'),
                                 feature_group_count=D_MODEL) + b_dw
    c = jax.nn.silu(_layernorm(c, ln_g, ln_b))
    return c @ w_pw2 + b_pw2, buf[:, -(CONV_WIDTH - 1):]


def _mla_project(h, pos, w_dq, g_q, w_uq, w_dkv, g_kv, w_uk):
    B, T, _ = h.shape
    inv = _mla_inv_freq()
    cq = _rmsnorm(h @ w_dq, g_q)
    q = (cq @ w_uq).reshape(B, T, MLA_HEADS, MLA_NOPE + MLA_ROPE)
    q_rope = _rope(q[..., MLA_NOPE:], pos, inv)
    q_lat = jnp.einsum('bthn,rhn->bthr', q[..., :MLA_NOPE], w_uk)
    a = h @ w_dkv
    c_kv = _rmsnorm(a[..., :MLA_KV_LORA], g_kv)
    k_rope = _rope(a[..., MLA_KV_LORA:][:, :, None, :], pos, inv)[:, :, 0, :]
    return q_lat, q_rope, c_kv, k_rope


def _mla_scores(q_lat, q_rope, c_kv, k_rope):
    s = (jnp.einsum('bthr,bsr->bhts', q_lat, c_kv).astype(jnp.float32)
         + jnp.einsum('bthe,bse->bhts', q_rope, k_rope).astype(jnp.float32))
    return s * MLA_SCALE


def _mla_attend_prompt(q_lat, q_rope, c_kv, k_rope):
    B, S = q_lat.shape[:2]
    nb = S // MLA_Q_BLOCK
    qb = q_lat.reshape(B, nb, MLA_Q_BLOCK, MLA_HEADS, MLA_KV_LORA).swapaxes(0, 1)
    rb = q_rope.reshape(B, nb, MLA_Q_BLOCK, MLA_HEADS, MLA_ROPE).swapaxes(0, 1)
    k_pos = jnp.arange(S)

    def block(args):
        q_l, q_r, i = args
        q_pos = i * MLA_Q_BLOCK + jnp.arange(MLA_Q_BLOCK)
        s = _mla_scores(q_l, q_r, c_kv, k_rope)
        s = jnp.where((k_pos[None, :] <= q_pos[:, None])[None, None], s, NEG_INF)
        p = jax.nn.softmax(s, axis=-1).astype(c_kv.dtype)
        return jnp.einsum('bhts,bsr->bthr', p, c_kv)

    out = lax.map(block, (qb, rb, jnp.arange(nb)))
    return out.swapaxes(0, 1).reshape(B, S, MLA_HEADS, MLA_KV_LORA)


def _mla_attend_sample(q_lat, q_rope, c_new, kr_new, c_past, kr_past):
    T = q_lat.shape[1]
    P = c_past.shape[1]
    s_past = _mla_scores(q_lat, q_rope, c_past, kr_past)
    s_new = _mla_scores(q_lat, q_rope, c_new, kr_new)
    causal = jnp.tril(jnp.ones((T, T), dtype=bool))
    s_new = jnp.where(causal[None, None], s_new, NEG_INF)
    p = jax.nn.softmax(jnp.concatenate([s_past, s_new], axis=-1), axis=-1).astype(c_new.dtype)
    return (jnp.einsum('bhts,bsr->bthr', p[..., :P], c_past)
            + jnp.einsum('bhts,bsr->bthr', p[..., P:], c_new))


def _mla_out(out_lat, w_uv, w_o):
    B, T = out_lat.shape[:2]
    o = jnp.einsum('bthr,rhv->bthv', out_lat, w_uv).reshape(B, T, MLA_HEADS * MLA_V)
    return o @ w_o


def _ret_log_gamma():
    return jnp.log(1.0 - 2.0 ** (-5.0 - jnp.arange(RET_HEADS, dtype=jnp.float32)))


def _ret_qkvg(h, pos, w_q, w_k, w_v, w_g):
    B, T, _ = h.shape
    inv = _ret_inv_freq()
    q = _rope((h @ w_q).reshape(B, T, RET_HEADS, RET_DK), pos, inv)
    k = _rope((h @ w_k).reshape(B, T, RET_HEADS, RET_DK), pos, inv) * (RET_DK ** -0.5)
    v = (h @ w_v).reshape(B, T, RET_HEADS, RET_DV)
    g = h @ w_g
    return q.astype(jnp.float32), k.astype(jnp.float32), v.astype(jnp.float32), g


def _ret_chunk(q, k, v, state, log_gamma):
    L = q.shape[1]
    n = jnp.arange(L, dtype=jnp.float32)
    diff = n[:, None] - n[None, :]
    decay = jnp.exp(jnp.where(diff[None] >= 0, diff[None] * log_gamma[:, None, None], -jnp.inf))
    inner = jnp.einsum('bnhd,bmhd->bhnm', q, k) * decay[None]
    o = jnp.einsum('bhnm,bmhe->bnhe', inner, v)
    cross = jnp.exp((n[:, None] + 1.0) * log_gamma[None, :])
    o = o + jnp.einsum('bnhd,bhde->bnhe', q, state) * cross[None, :, :, None]
    k_dec = k * jnp.exp((L - 1.0 - n)[:, None] * log_gamma[None, :])[None, :, :, None]
    new_state = (state * jnp.exp(L * log_gamma)[None, :, None, None]
                 + jnp.einsum('bmhd,bmhe->bhde', k_dec, v))
    return o, new_state


def _ret_out(o, g, gn_g, gn_b, w_o):
    B, T = o.shape[:2]
    mu = jnp.mean(o, axis=-1, keepdims=True)
    oc = o - mu
    on = oc * lax.rsqrt(jnp.mean(oc * oc, axis=-1, keepdims=True) + EPS)
    on = on.reshape(B, T, RET_HEADS * RET_DV) * gn_g.astype(jnp.float32) + gn_b.astype(jnp.float32)
    return (jax.nn.silu(g) * on.astype(g.dtype)) @ w_o


def setup_inputs(seed: int = 0) -> dict:
    key = jax.random.key(seed)
    ks = iter(jax.random.split(key, 64))
    f32 = jnp.float32

    def nrm(shape, fan_in):
        return jax.random.normal(next(ks), shape, f32) * (fan_in ** -0.5)

    def gain(shape):
        return 1.0 + 0.05 * jax.random.normal(next(ks), shape, f32)

    def bias(shape):
        return 0.02 * jax.random.normal(next(ks), shape, f32)

    n_pages = PAST_LEN // PAGE_SIZE
    n_used = DEC_BATCH * n_pages
    n_phys = n_used + (n_used + 3) // 4
    perm = jax.random.permutation(next(ks), n_phys)
    page_table = perm[:n_used].reshape(DEC_BATCH, n_pages).astype(jnp.int32)

    return {
        'x_prompt': jax.random.normal(next(ks), (BATCH, SEQ, D_MODEL), f32),
        'x_sample': jax.random.normal(next(ks), (DEC_BATCH, DEC_SEQ, D_MODEL), f32),
        'state_conv': jax.random.normal(next(ks), (N_CONV_LAYERS, DEC_BATCH, CONV_WIDTH - 1, D_MODEL), f32),
        'cache_mla_latent': jax.random.normal(next(ks), (N_MLA_LAYERS, n_phys, PAGE_SIZE, MLA_KV_LORA), f32),
        'cache_mla_krope': jax.random.normal(next(ks), (N_MLA_LAYERS, n_phys, PAGE_SIZE, MLA_ROPE), f32),
        'state_ret': jax.random.normal(next(ks), (N_RET_LAYERS, DEC_BATCH, RET_HEADS, RET_DK, RET_DV), f32),
        'page_table': page_table,
        'norm_mix': gain((DEPTH, D_MODEL)),
        'norm_ffn': gain((DEPTH, D_MODEL)),
        'norm_final': gain((D_MODEL,)),
        'conv_w_pw1': nrm((N_CONV_LAYERS, D_MODEL, 2 * D_MODEL), D_MODEL),
        'conv_b_pw1': bias((N_CONV_LAYERS, 2 * D_MODEL)),
        'conv_w_dw': nrm((N_CONV_LAYERS, CONV_WIDTH, D_MODEL), CONV_WIDTH),
        'conv_b_dw': bias((N_CONV_LAYERS, D_MODEL)),
        'conv_ln_g': gain((N_CONV_LAYERS, D_MODEL)),
        'conv_ln_b': bias((N_CONV_LAYERS, D_MODEL)),
        'conv_w_pw2': nrm((N_CONV_LAYERS, D_MODEL, D_MODEL), D_MODEL),
        'conv_b_pw2': bias((N_CONV_LAYERS, D_MODEL)),
        'mla_w_dq': nrm((N_MLA_LAYERS, D_MODEL, MLA_Q_LORA), D_MODEL),
        'mla_g_q': gain((N_MLA_LAYERS, MLA_Q_LORA)),
        'mla_w_uq': nrm((N_MLA_LAYERS, MLA_Q_LORA, MLA_HEADS * (MLA_NOPE + MLA_ROPE)), MLA_Q_LORA),
        'mla_w_dkv': nrm((N_MLA_LAYERS, D_MODEL, MLA_KV_LORA + MLA_ROPE), D_MODEL),
        'mla_g_kv': gain((N_MLA_LAYERS, MLA_KV_LORA)),
        'mla_w_uk': nrm((N_MLA_LAYERS, MLA_KV_LORA, MLA_HEADS, MLA_NOPE), MLA_KV_LORA),
        'mla_w_uv': nrm((N_MLA_LAYERS, MLA_KV_LORA, MLA_HEADS, MLA_V), MLA_KV_LORA),
        'mla_w_o': nrm((N_MLA_LAYERS, MLA_HEADS * MLA_V, D_MODEL), MLA_HEADS * MLA_V),
        'ret_w_q': nrm((N_RET_LAYERS, D_MODEL, RET_HEADS * RET_DK), D_MODEL),
        'ret_w_k': nrm((N_RET_LAYERS, D_MODEL, RET_HEADS * RET_DK), D_MODEL),
        'ret_w_v': nrm((N_RET_LAYERS, D_MODEL, RET_HEADS * RET_DV), D_MODEL),
        'ret_w_g': nrm((N_RET_LAYERS, D_MODEL, RET_HEADS * RET_DV), D_MODEL),
        'ret_gn_g': gain((N_RET_LAYERS, RET_HEADS * RET_DV)),
        'ret_gn_b': bias((N_RET_LAYERS, RET_HEADS * RET_DV)),
        'ret_w_o': nrm((N_RET_LAYERS, RET_HEADS * RET_DV, D_MODEL), RET_HEADS * RET_DV),
        'ffn_w1': nrm((DEPTH, D_MODEL, D_FF), D_MODEL),
        'ffn_w3': nrm((DEPTH, D_MODEL, D_FF), D_MODEL),
        'ffn_w2': nrm((DEPTH, D_FF, D_MODEL), D_FF),
    }


def reference(x_prompt, x_sample, state_conv, cache_mla_latent, cache_mla_krope, state_ret, page_table,
              norm_mix, norm_ffn, norm_final,
              conv_w_pw1, conv_b_pw1, conv_w_dw, conv_b_dw, conv_ln_g, conv_ln_b, conv_w_pw2, conv_b_pw2,
              mla_w_dq, mla_g_q, mla_w_uq, mla_w_dkv, mla_g_kv, mla_w_uk, mla_w_uv, mla_w_o,
              ret_w_q, ret_w_k, ret_w_v, ret_w_g, ret_gn_g, ret_gn_b, ret_w_o,
              ffn_w1, ffn_w3, ffn_w2):
    B, S, _ = x_prompt.shape
    DB, T, _ = x_sample.shape
    past_len = page_table.shape[1] * PAGE_SIZE
    pos_p = jnp.arange(S)
    pos_s = past_len + jnp.arange(T)
    log_gamma = _ret_log_gamma()

    xp, xs = x_prompt, x_sample
    conv_p, conv_s = [], []
    lat_p, kr_p, lat_s, kr_s = [], [], [], []
    ret_p, ret_s = [], []

    for i in range(DEPTH):
        j = i // N_MIXERS
        kind = i % N_MIXERS
        hp = _rmsnorm(xp, norm_mix[i])
        hs = _rmsnorm(xs, norm_mix[i])
        if kind == 0:
            cw = (conv_w_pw1[j], conv_b_pw1[j], conv_w_dw[j], conv_b_dw[j],
                  conv_ln_g[j], conv_ln_b[j], conv_w_pw2[j], conv_b_pw2[j])
            mp, bp = _conv_module(hp, jnp.zeros((B, CONV_WIDTH - 1, D_MODEL), hp.dtype), *cw)
            ms, bs = _conv_module(hs, state_conv[j], *cw)
            conv_p.append(bp)
            conv_s.append(bs.astype(state_conv.dtype))
        elif kind == 1:
            pw = (mla_w_dq[j], mla_g_q[j], mla_w_uq[j], mla_w_dkv[j], mla_g_kv[j], mla_w_uk[j])
            ql, qr, ck, kr = _mla_project(hp, pos_p, *pw)
            mp = _mla_out(_mla_attend_prompt(ql, qr, ck, kr), mla_w_uv[j], mla_w_o[j])
            lat_p.append(ck)
            kr_p.append(kr)
            ql, qr, ck, kr = _mla_project(hs, pos_s, *pw)
            c_past = cache_mla_latent[j][page_table].reshape(DB, past_len, MLA_KV_LORA)
            r_past = cache_mla_krope[j][page_table].reshape(DB, past_len, MLA_ROPE)
            ms = _mla_out(_mla_attend_sample(ql, qr, ck, kr, c_past.astype(ck.dtype), r_past.astype(kr.dtype)),
                          mla_w_uv[j], mla_w_o[j])
            lat_s.append(ck)
            kr_s.append(kr)
        else:
            rw = (ret_w_q[j], ret_w_k[j], ret_w_v[j], ret_w_g[j])
            q, k, v, g = _ret_qkvg(hp, pos_p, *rw)
            nc = S // RET_CHUNK

            def chunks(a):
                return a.reshape(B, nc, RET_CHUNK, *a.shape[2:]).swapaxes(0, 1)

            def step(st, qkv):
                o_c, st = _ret_chunk(qkv[0], qkv[1], qkv[2], st, log_gamma)
                return st, o_c

            st0 = jnp.zeros((B, RET_HEADS, RET_DK, RET_DV), jnp.float32)
            st_p, o = lax.scan(step, st0, (chunks(q), chunks(k), chunks(v)))
            o = o.swapaxes(0, 1).reshape(B, S, RET_HEADS, RET_DV)
            mp = _ret_out(o, g, ret_gn_g[j], ret_gn_b[j], ret_w_o[j])
            ret_p.append(st_p.astype(xp.dtype))
            q, k, v, g = _ret_qkvg(hs, pos_s, *rw)
            o, st_s = _ret_chunk(q, k, v, state_ret[j].astype(jnp.float32), log_gamma)
            ms = _ret_out(o, g, ret_gn_g[j], ret_gn_b[j], ret_w_o[j])
            ret_s.append(st_s.astype(state_ret.dtype))
        xp = xp + mp.astype(xp.dtype)
        xs = xs + ms.astype(xs.dtype)
        xp = xp + _swiglu(_rmsnorm(xp, norm_ffn[i]), ffn_w1[i], ffn_w3[i], ffn_w2[i])
        xs = xs + _swiglu(_rmsnorm(xs, norm_ffn[i]), ffn_w1[i], ffn_w3[i], ffn_w2[i])

    y_prompt = _rmsnorm(xp, norm_final)
    y_sample = _rmsnorm(xs, norm_final)
    return (y_prompt, y_sample,
            jnp.stack(conv_p), jnp.stack(conv_s),
            jnp.stack(lat_p), jnp.stack(kr_p), jnp.stack(lat_s), jnp.stack(kr_s),
            jnp.stack(ret_p), jnp.stack(ret_s))
```

```python
import functools
import math

import jax
import jax.numpy as jnp
from jax import lax
from jax.experimental import pallas as pl
from jax.experimental.pallas import tpu as pltpu

F32 = jnp.float32
BF16 = jnp.bfloat16

EPS = 1e-6
NEG_INF = -1e30
ROPE_BASE = 10000.0
PAGE_SIZE = 128
N_MIXERS = 3
CONV_WIDTH = 31
CONV_PREV = CONV_WIDTH - 1
MLA_HEADS = 8
MLA_KV_LORA = 256
MLA_NOPE = 128
MLA_ROPE = 64
MLA_V = 128
MLA_QK = MLA_KV_LORA + 128
MLA_SCALE = (MLA_NOPE + MLA_ROPE) ** -0.5
RET_DK = 256
RET_DV = 512

LANE = 128
MIB = 1024 * 1024


def _cparams(sem, vmem_mib):
    return pltpu.CompilerParams(dimension_semantics=sem, vmem_limit_bytes=vmem_mib * MIB)


def _const_spec(shape):
    nd = len(shape)
    return pl.BlockSpec(shape, lambda *_: (0,) * nd, pipeline_mode=pl.Buffered(1))


def _mm(a, b):
    return jnp.dot(a, b, preferred_element_type=F32)


def _mm_nt(a, b):
    return lax.dot_general(a, b, (((1,), (1,)), ((), ())), preferred_element_type=F32)


def _rms(x, g):
    return x * lax.rsqrt(jnp.mean(x * x, axis=-1, keepdims=True) + EPS) * g


def _silu(x):
    return x * jax.nn.sigmoid(x)


def _ffn_kernel(x_ref, g_ref, w1_ref, w3_ref, w2_ref, *rest, final):
    if final:
        gf_ref, o_ref, h_ref, acc_ref = rest
    else:
        o_ref, h_ref, acc_ref = rest
    k = pl.program_id(1)

    @pl.when(k == 0)
    def _():
        x = x_ref[...]
        h_ref[...] = _rms(x, g_ref[...]).astype(BF16)
        acc_ref[...] = x

    h = h_ref[...]
    a = _mm(h, w1_ref[...])
    b = _mm(h, w3_ref[...])
    acc_ref[...] += _mm((_silu(a) * b).astype(BF16), w2_ref[...])

    @pl.when(k == pl.num_programs(1) - 1)
    def _():
        y = acc_ref[...]
        if final:
            y = _rms(y, gf_ref[...])
        o_ref[...] = y


def _ffn(x, g, w1, w3, w2, g_final=None, *, tm, n_ff_chunks=2):
    n, d = x.shape
    dff = w1.shape[1]
    tf = dff // n_ff_chunks
    assert n % tm == 0 and dff % n_ff_chunks == 0 and tf % LANE == 0
    final = g_final is not None
    in_specs = [
        pl.BlockSpec((tm, d), lambda i, k: (i, 0)),
        _const_spec((1, d)),
        pl.BlockSpec((d, tf), lambda i, k: (0, k)),
        pl.BlockSpec((d, tf), lambda i, k: (0, k)),
        pl.BlockSpec((tf, d), lambda i, k: (k, 0)),
    ]
    args = [x, g.reshape(1, d), w1, w3, w2]
    if final:
        in_specs.append(_const_spec((1, d)))
        args.append(g_final.reshape(1, d))
    return pl.pallas_call(
        functools.partial(_ffn_kernel, final=final),
        grid=(n // tm, n_ff_chunks),
        in_specs=in_specs,
        out_specs=pl.BlockSpec((tm, d), lambda i, k: (i, 0)),
        out_shape=jax.ShapeDtypeStruct((n, d), F32),
        scratch_shapes=[pltpu.VMEM((tm, d), BF16), pltpu.VMEM((tm, d), F32)],
        compiler_params=_cparams(("parallel", "arbitrary"), 48),
        name="ffn_final" if final else "ffn",
    )(*args)


def _conv_glu(x, gm, w1, b1):
    d = x.shape[-1]
    a = _mm(_rms(x, gm).astype(BF16), w1) + b1
    return a[:, :d] * jax.nn.sigmoid(a[:, d:])


def _conv_tail(c, lng, lnb, w2, b2):
    mu = jnp.mean(c, axis=-1, keepdims=True)
    cc = c - mu
    cn = cc * lax.rsqrt(jnp.mean(cc * cc, axis=-1, keepdims=True) + EPS) * lng + lnb
    return _mm(_silu(cn).astype(BF16), w2) + b2


CONV_CARRY = 32
CONV_ROWS = 64
CONV_COLS = 256


def _conv_prompt_kernel(x_ref, gm_ref, w1_ref, b1_ref, wdw_ref, bdw_ref, lng_ref, lnb_ref, w2_ref, b2_ref,
                        o_ref, st_ref, ubuf_ref, c_ref):
    tt, d = x_ref.shape
    t = pl.program_id(1)

    @pl.when(t == 0)
    def _():
        ubuf_ref[0:CONV_CARRY, :] = jnp.zeros((CONV_CARRY, d), F32)

    x = x_ref[...]
    ubuf_ref[CONV_CARRY:CONV_CARRY + tt, :] = _conv_glu(x, gm_ref[...], w1_ref[...], b1_ref[...])

    lead = CONV_CARRY - CONV_PREV
    for r0 in range(0, tt, CONV_ROWS):
        for c0 in range(0, d, CONV_COLS):
            cols = slice(c0, c0 + CONV_COLS)
            acc = jnp.broadcast_to(bdw_ref[:, cols], (CONV_ROWS, CONV_COLS))
            for k in range(CONV_WIDTH):
                acc = acc + ubuf_ref[r0 + lead + k:r0 + lead + k + CONV_ROWS, cols] * wdw_ref[k:k + 1, cols]
            c_ref[r0:r0 + CONV_ROWS, cols] = acc

    y = _conv_tail(c_ref[...], lng_ref[...], lnb_ref[...], w2_ref[...], b2_ref[...])
    o_ref[...] = x + y
    st_ref[...] = ubuf_ref[tt + lead:tt + CONV_CARRY, :]
    ubuf_ref[0:CONV_CARRY, :] = ubuf_ref[tt:tt + CONV_CARRY, :]


def _conv_prompt(x, seq, gm, w1, b1, wdw, bdw, lng, lnb, w2, b2, *, tt):
    n, d = x.shape
    batch = n // seq
    nt = seq // tt
    assert seq % tt == 0 and tt % CONV_ROWS == 0 and d % CONV_COLS == 0
    row = lambda a: a.reshape(1, -1)
    return pl.pallas_call(
        _conv_prompt_kernel,
        grid=(batch, nt),
        in_specs=[
            pl.BlockSpec((tt, d), lambda b, t: (b * nt + t, 0)),
            _const_spec((1, d)), _const_spec((d, 2 * d)), _const_spec((1, 2 * d)),
            _const_spec((CONV_WIDTH, d)), _const_spec((1, d)), _const_spec((1, d)), _const_spec((1, d)),
            _const_spec((d, d)), _const_spec((1, d)),
        ],
        out_specs=[
            pl.BlockSpec((tt, d), lambda b, t: (b * nt + t, 0)),
            pl.BlockSpec((None, CONV_PREV, d), lambda b, t: (b, 0, 0)),
        ],
        out_shape=[jax.ShapeDtypeStruct((n, d), F32), jax.ShapeDtypeStruct((batch, CONV_PREV, d), F32)],
        scratch_shapes=[pltpu.VMEM((tt + CONV_CARRY, d), F32), pltpu.VMEM((tt, d), F32)],
        compiler_params=_cparams(("parallel", "arbitrary"), 48),
        name="conv_prompt",
    )(x, row(gm), w1, row(b1), wdw, row(bdw), row(lng), row(lnb), w2, row(b2))


def _conv_sample_kernel(x_ref, st_ref, gm_ref, w1_ref, b1_ref, wdw_ref, bdw_ref, lng_ref, lnb_ref, w2_ref, b2_ref,
                        o_ref, ns_ref):
    nt, bb, d = x_ref.shape
    x = x_ref[...].reshape(nt * bb, d)
    u = _conv_glu(x, gm_ref[...], w1_ref[...], b1_ref[...])
    us = [u[i * bb:(i + 1) * bb] for i in range(nt)]

    def buf(j):
        if j < CONV_PREV:
            return st_ref[:, j * d:(j + 1) * d]
        return us[j - CONV_PREV]

    cs = []
    for t in range(nt):
        acc = jnp.broadcast_to(bdw_ref[...], (bb, d))
        for k in range(CONV_WIDTH):
            acc = acc + buf(t + k) * wdw_ref[k:k + 1, :]
        cs.append(acc)
    c = jnp.concatenate(cs, axis=0)
    y = _conv_tail(c, lng_ref[...], lnb_ref[...], w2_ref[...], b2_ref[...])
    o_ref[...] = (x + y).reshape(nt, bb, d)
    keep = CONV_PREV - nt
    ns_ref[:, 0:keep * d] = st_ref[:, nt * d:CONV_PREV * d]
    for i in range(nt):
        ns_ref[:, (keep + i) * d:(keep + i + 1) * d] = us[i]


def _conv_sample(x3, state2, gm, w1, b1, wdw, bdw, lng, lnb, w2, b2, *, bb):
    nt, db, d = x3.shape
    assert db % bb == 0 and nt <= CONV_PREV
    row = lambda a: a.reshape(1, -1)
    return pl.pallas_call(
        _conv_sample_kernel,
        grid=(db // bb,),
        in_specs=[
            pl.BlockSpec((nt, bb, d), lambda i: (0, i, 0)),
            pl.BlockSpec((bb, CONV_PREV * d), lambda i: (i, 0)),
            _const_spec((1, d)), _const_spec((d, 2 * d)), _const_spec((1, 2 * d)),
            _const_spec((CONV_WIDTH, d)), _const_spec((1, d)), _const_spec((1, d)), _const_spec((1, d)),
            _const_spec((d, d)), _const_spec((1, d)),
        ],
        out_specs=[
            pl.BlockSpec((nt, bb, d), lambda i: (0, i, 0)),
            pl.BlockSpec((bb, CONV_PREV * d), lambda i: (i, 0)),
        ],
        out_shape=[jax.ShapeDtypeStruct((nt, db, d), F32), jax.ShapeDtypeStruct((db, CONV_PREV * d), F32)],
        compiler_params=_cparams(("parallel",), 48),
        name="conv_sample",
    )(x3, state2, row(gm), w1, row(b1), wdw, row(bdw), row(lng), row(lnb), w2, row(b2))


def _mla_proj_kernel(x_ref, gm_ref, wdq_ref, gq_ref, wuq_ref, wdkv_ref, gkv_ref, wuk_ref, cos_ref, sin_ref,
                     q_ref, k_ref, ckv_ref, kr_ref):
    h = _rms(x_ref[...], gm_ref[...]).astype(BF16)
    cos = cos_ref[...]
    sin = sin_ref[...]
    cq = _rms(_mm(h, wdq_ref[...]), gq_ref[...]).astype(BF16)
    q = _mm(cq, wuq_ref[...])
    nh = q_ref.shape[0]
    for i in range(nh):
        base = i * 3 * LANE
        q_nope = q[:, base:base + LANE].astype(BF16)
        q_lat = _mm(q_nope, wuk_ref[i])
        q_rot = q[:, base + LANE:base + 2 * LANE] * cos + q[:, base + 2 * LANE:base + 3 * LANE] * sin
        q_ref[i, :, 0:MLA_KV_LORA] = (q_lat * MLA_SCALE).astype(BF16)
        q_ref[i, :, MLA_KV_LORA:MLA_QK] = (q_rot * MLA_SCALE).astype(BF16)
    a = _mm(h, wdkv_ref[...])
    ckv = _rms(a[:, 0:MLA_KV_LORA], gkv_ref[...])
    k_rot = a[:, MLA_KV_LORA:MLA_QK] * cos + a[:, MLA_QK:MLA_QK + LANE] * sin
    ckv_ref[...] = ckv
    kr_ref[...] = k_rot[:, 0:MLA_ROPE]
    k_ref[:, 0:MLA_KV_LORA] = ckv.astype(BF16)
    k_ref[:, MLA_KV_LORA:MLA_QK] = k_rot.astype(BF16)


def _mla_proj(x, gm, w, cos, sin, *, tm):
    n, d = x.shape
    period = cos.shape[0] // tm
    assert n % tm == 0 and cos.shape[0] % tm == 0
    ql = w["dq"].shape[1]
    return pl.pallas_call(
        _mla_proj_kernel,
        grid=(n // tm,),
        in_specs=[
            pl.BlockSpec((tm, d), lambda i: (i, 0)),
            _const_spec((1, d)), _const_spec((d, ql)), _const_spec((1, ql)),
            _const_spec(w["uq"].shape), _const_spec(w["dkv"].shape), _const_spec((1, MLA_KV_LORA)),
            _const_spec(w["uk"].shape),
            pl.BlockSpec((tm, LANE), lambda i: (i % period, 0)),
            pl.BlockSpec((tm, LANE), lambda i: (i % period, 0)),
        ],
        out_specs=[
            pl.BlockSpec((MLA_HEADS, tm, MLA_QK), lambda i: (0, i, 0)),
            pl.BlockSpec((tm, MLA_QK), lambda i: (i, 0)),
            pl.BlockSpec((tm, MLA_KV_LORA), lambda i: (i, 0)),
            pl.BlockSpec((tm, MLA_ROPE), lambda i: (i, 0)),
        ],
        out_shape=[
            jax.ShapeDtypeStruct((MLA_HEADS, n, MLA_QK), BF16),
            jax.ShapeDtypeStruct((n, MLA_QK), BF16),
            jax.ShapeDtypeStruct((n, MLA_KV_LORA), F32),
            jax.ShapeDtypeStruct((n, MLA_ROPE), F32),
        ],
        compiler_params=_cparams(("parallel",), 48),
        name="mla_proj",
    )(x, gm.reshape(1, d), w["dq"], w["gq"].reshape(1, ql), w["uq"], w["dkv"], w["gkv"].reshape(1, MLA_KV_LORA),
      w["uk"], cos, sin)


def _mla_attn_prompt_kernel(qi_ref, kj_ref, q_ref, k_ref, o_ref, m_ref, l_ref, acc_ref):
    nh, tq, dq = q_ref.shape
    tk = k_ref.shape[0]
    p = pl.program_id(1)
    i = qi_ref[p]
    j = kj_ref[p]

    @pl.when(j == 0)
    def _():
        m_ref[...] = jnp.full(m_ref.shape, NEG_INF, F32)
        l_ref[...] = jnp.zeros(l_ref.shape, F32)
        acc_ref[...] = jnp.zeros(acc_ref.shape, F32)

    k = k_ref[...]
    s = _mm_nt(q_ref[...].reshape(nh * tq, dq), k)

    def update(s):
        m_prev = m_ref[...]
        m_new = jnp.maximum(m_prev, jnp.max(s, axis=-1, keepdims=True))
        alpha = jnp.exp(m_prev - m_new)
        e = jnp.exp(s - m_new)
        l_ref[...] = alpha * l_ref[...] + jnp.sum(e, axis=-1, keepdims=True)
        acc_ref[...] = alpha * acc_ref[...] + _mm(e.astype(BF16), k[:, 0:MLA_KV_LORA])
        m_ref[...] = m_new

    @pl.when(j < i)
    def _():
        update(s)

    @pl.when(j == i)
    def _():
        causal = lax.broadcasted_iota(jnp.int32, (tq, tk), 1) <= lax.broadcasted_iota(jnp.int32, (tq, tk), 0)
        update(jnp.where(causal[None], s.reshape(nh, tq, tk), NEG_INF).reshape(nh * tq, tk))
        o_ref[...] = (acc_ref[...] / l_ref[...]).reshape(nh, tq, MLA_KV_LORA).astype(o_ref.dtype)


def _mla_attn_prompt(q, k, seq, *, tq):
    nh, n, dq = q.shape
    batch = n // seq
    nb = seq // tq
    assert seq % tq == 0
    pairs = [(i, j) for i in range(nb) for j in range(i + 1)]
    qi = jnp.asarray([p[0] for p in pairs], jnp.int32)
    kj = jnp.asarray([p[1] for p in pairs], jnp.int32)
    rows = nh * tq
    grid_spec = pltpu.PrefetchScalarGridSpec(
        num_scalar_prefetch=2,
        grid=(batch, len(pairs)),
        in_specs=[
            pl.BlockSpec((nh, tq, dq), lambda b, p, qi, kj: (0, b * nb + qi[p], 0)),
            pl.BlockSpec((tq, dq), lambda b, p, qi, kj: (b * nb + kj[p], 0)),
        ],
        out_specs=pl.BlockSpec((nh, tq, MLA_KV_LORA), lambda b, p, qi, kj: (0, b * nb + qi[p], 0)),
        scratch_shapes=[pltpu.VMEM((rows, 1), F32), pltpu.VMEM((rows, 1), F32),
                        pltpu.VMEM((rows, MLA_KV_LORA), F32)],
    )
    return pl.pallas_call(
        _mla_attn_prompt_kernel,
        grid_spec=grid_spec,
        out_shape=jax.ShapeDtypeStruct((nh, n, MLA_KV_LORA), BF16),
        compiler_params=_cparams(("parallel", "arbitrary"), 48),
        name="mla_attn_prompt",
    )(qi, kj, q, k)


MLA_SAMPLE_CHUNK = 1024
MLA_NEW_PAD = 128


def _mla_attn_sample_kernel(pt_ref, q_ref, kn_ref, lat_hbm, kr_hbm, o_ref, lat_buf, kr_buf, s_ref, sem, *, n_new):
    b = pl.program_id(0)
    nb = pl.num_programs(0)
    past = lat_buf.shape[1]
    n_pages = past // PAGE_SIZE
    rows = q_ref.shape[0]
    slot = b % 2

    def page_copies(bi, sl, pg):
        page = pt_ref[bi * n_pages + pg]
        dst = pl.ds(pg * PAGE_SIZE, PAGE_SIZE)
        return (pltpu.make_async_copy(lat_hbm.at[page], lat_buf.at[sl, dst], sem.at[0, sl]),
                pltpu.make_async_copy(kr_hbm.at[page], kr_buf.at[sl, dst], sem.at[1, sl]))

    def start_fetch(bi, sl):
        def body(pg, _):
            for c in page_copies(bi, sl, pg):
                c.start()
            return 0
        lax.fori_loop(0, n_pages, body, 0)

    @pl.when(b == 0)
    def _():
        start_fetch(0, 0)

    @pl.when(b + 1 < nb)
    def _():
        start_fetch(b + 1, 1 - slot)

    def wait_body(pg, _):
        for c in page_copies(b, slot, pg):
            c.wait()
        return 0
    lax.fori_loop(0, n_pages, wait_body, 0)

    q = q_ref[...]
    q_lat = q[:, 0:MLA_KV_LORA]
    q_rope = q[:, MLA_KV_LORA:MLA_KV_LORA + MLA_ROPE]
    ch = MLA_SAMPLE_CHUNK
    n_chunks = past // ch

    def score_body(c, m):
        off = pl.multiple_of(c * ch, ch)
        s = (_mm_nt(q_lat, lat_buf[slot, pl.ds(off, ch), :].astype(BF16))
             + _mm_nt(q_rope, kr_buf[slot, pl.ds(off, ch), :].astype(BF16)))
        s_ref[:, pl.ds(off, ch)] = s
        return jnp.maximum(m, jnp.max(s, axis=-1, keepdims=True))

    m = lax.fori_loop(0, n_chunks, score_body, jnp.full((rows, 1), NEG_INF, F32))

    kn = kn_ref[...]
    k_new = jnp.concatenate([kn, jnp.zeros((MLA_NEW_PAD - kn.shape[0], kn.shape[1]), F32)], axis=0).astype(BF16)
    s_new = _mm_nt(q, k_new)
    r_tok = lax.broadcasted_iota(jnp.int32, (rows, MLA_NEW_PAD), 0) % n_new
    s_new = jnp.where(lax.broadcasted_iota(jnp.int32, (rows, MLA_NEW_PAD), 1) <= r_tok, s_new, NEG_INF)
    m = jnp.maximum(m, jnp.max(s_new, axis=-1, keepdims=True))
    e_new = jnp.exp(s_new - m)
    l0 = jnp.sum(e_new, axis=-1, keepdims=True)
    acc0 = _mm(e_new.astype(BF16), k_new[:, 0:MLA_KV_LORA])

    def out_body(c, carry):
        l, acc = carry
        off = pl.multiple_of(c * ch, ch)
        e = jnp.exp(s_ref[:, pl.ds(off, ch)] - m)
        acc = acc + _mm(e.astype(BF16), lat_buf[slot, pl.ds(off, ch), :].astype(BF16))
        return l + jnp.sum(e, axis=-1, keepdims=True), acc

    l, acc = lax.fori_loop(0, n_chunks, out_body, (l0, acc0))
    o_ref[...] = (acc / l).astype(o_ref.dtype)


def _mla_attn_sample(q, k_new, cache_lat, cache_kr, page_table, n_new):
    db, rows, dq = q.shape
    tp = k_new.shape[1]
    n_pages = page_table.shape[1]
    past = n_pages * PAGE_SIZE
    assert past % MLA_SAMPLE_CHUNK == 0
    grid_spec = pltpu.PrefetchScalarGridSpec(
        num_scalar_prefetch=1,
        grid=(db,),
        in_specs=[
            pl.BlockSpec((None, rows, dq), lambda b, pt: (b, 0, 0)),
            pl.BlockSpec((None, tp, dq), lambda b, pt: (b, 0, 0)),
            pl.BlockSpec(memory_space=pl.ANY),
            pl.BlockSpec(memory_space=pl.ANY),
        ],
        out_specs=pl.BlockSpec((None, rows, MLA_KV_LORA), lambda b, pt: (b, 0, 0)),
        scratch_shapes=[
            pltpu.VMEM((2, past, MLA_KV_LORA), F32),
            pltpu.VMEM((2, past, MLA_ROPE), F32),
            pltpu.VMEM((rows, past), F32),
            pltpu.SemaphoreType.DMA((2, 2)),
        ],
    )
    return pl.pallas_call(
        functools.partial(_mla_attn_sample_kernel, n_new=n_new),
        grid_spec=grid_spec,
        out_shape=jax.ShapeDtypeStruct((db, rows, MLA_KV_LORA), BF16),
        compiler_params=_cparams(("arbitrary",), 48),
        name="mla_attn_sample",
    )(page_table.reshape(-1), q, k_new, cache_lat, cache_kr)


def _mla_out_kernel(x_ref, a_ref, wuv_ref, wo_ref, o_ref):
    nh = a_ref.shape[0]
    o = jnp.concatenate([_mm(a_ref[i], wuv_ref[i]).astype(BF16) for i in range(nh)], axis=-1)
    o_ref[...] = x_ref[...] + _mm(o, wo_ref[...])


def _mla_out(x, a, wuv, wo, *, tm):
    n, d = x.shape
    assert n % tm == 0
    return pl.pallas_call(
        _mla_out_kernel,
        grid=(n // tm,),
        in_specs=[
            pl.BlockSpec((tm, d), lambda i: (i, 0)),
            pl.BlockSpec((MLA_HEADS, tm, MLA_KV_LORA), lambda i: (0, i, 0)),
            _const_spec(wuv.shape), _const_spec(wo.shape),
        ],
        out_specs=pl.BlockSpec((tm, d), lambda i: (i, 0)),
        out_shape=jax.ShapeDtypeStruct((n, d), F32),
        compiler_params=_cparams(("parallel",), 48),
        name="mla_out",
    )(x, a, wuv, wo)


def _ret_project(x, gm, wq, wk, wv, wg, cos, sin, n_heads):
    h = _rms(x, gm).astype(BF16)
    half = RET_DK // 2

    def rot(a):
        outs = []
        for i in range(n_heads):
            a1 = a[:, i * RET_DK:i * RET_DK + half]
            a2 = a[:, i * RET_DK + half:(i + 1) * RET_DK]
            outs += [a1 * cos - a2 * sin, a2 * cos + a1 * sin]
        return jnp.concatenate(outs, axis=-1)

    return rot(_mm(h, wq)), rot(_mm(h, wk)) * (RET_DK ** -0.5), _mm(h, wv), _mm(h, wg)


def _ret_gate(o, g, gng, gnb):
    mu = jnp.mean(o, axis=-1, keepdims=True)
    oc = o - mu
    on = oc * lax.rsqrt(jnp.mean(oc * oc, axis=-1, keepdims=True) + EPS)
    return (_silu(g) * (on * gng + gnb)).astype(BF16)


def _ret_prompt_kernel(x_ref, gm_ref, wq_ref, wk_ref, wv_ref, wg_ref, cos_ref, sin_ref,
                       decay_ref, cross_ref, kdec_ref, gl_ref, gng_ref, gnb_ref, wo_ref,
                       o_ref, st_ref, state_ref, y_ref):
    tt, d = x_ref.shape
    nh, lc, _ = decay_ref.shape
    c = pl.program_id(1)

    @pl.when(c == 0)
    def _():
        state_ref[...] = jnp.zeros(state_ref.shape, F32)

    x = x_ref[...]
    q, k, v, g = _ret_project(x, gm_ref[...], wq_ref[...], wk_ref[...], wv_ref[...], wg_ref[...],
                              cos_ref[...], sin_ref[...], nh)
    y_ref[...] = x
    for r0 in range(0, tt, lc):
        rows = slice(r0, r0 + lc)
        for i in range(nh):
            qh = q[rows, i * RET_DK:(i + 1) * RET_DK].astype(BF16)
            kh = k[rows, i * RET_DK:(i + 1) * RET_DK]
            vh = v[rows, i * RET_DV:(i + 1) * RET_DV].astype(BF16)
            state = state_ref[i]
            inner = _mm_nt(qh, kh.astype(BF16)) * decay_ref[i]
            o = _mm(inner.astype(BF16), vh) + _mm(qh, state.astype(BF16)) * cross_ref[i]
            k_dec = (kh * kdec_ref[i]).T.astype(BF16)
            state_ref[i] = state * gl_ref[i] + _mm(k_dec, vh)
            hv = slice(i * RET_DV, (i + 1) * RET_DV)
            gated = _ret_gate(o, g[rows, hv], gng_ref[:, hv], gnb_ref[:, hv])
            y_ref[rows, :] += _mm(gated, wo_ref[hv, :])
    o_ref[...] = y_ref[...]

    @pl.when(c == pl.num_programs(1) - 1)
    def _():
        st_ref[...] = state_ref[...]


def _ret_tables(n_heads, lc):
    log_gamma = jnp.log(1.0 - 2.0 ** (-5.0 - jnp.arange(n_heads, dtype=F32)))
    n = jnp.arange(lc, dtype=F32)
    diff = n[:, None] - n[None, :]
    decay = jnp.exp(jnp.where(diff[None] >= 0, diff[None] * log_gamma[:, None, None], -jnp.inf))
    cross = jnp.exp((n[None, :] + 1.0) * log_gamma[:, None])[:, :, None]
    kdec = jnp.exp((lc - 1.0 - n)[None, :] * log_gamma[:, None])[:, :, None]
    gl = jnp.exp(lc * log_gamma)
    return decay, cross, kdec, gl


def _ret_prompt(x, seq, gm, w, cos, sin, *, tt, lc):
    n, d = x.shape
    batch = n // seq
    nt = seq // tt
    nh = w["q"].shape[1] // RET_DK
    assert seq % tt == 0 and tt % lc == 0
    decay, cross, kdec, gl = _ret_tables(nh, lc)
    gl = jnp.broadcast_to(gl[:, None, None], (nh, 1, RET_DV))
    dv = nh * RET_DV
    return pl.pallas_call(
        _ret_prompt_kernel,
        grid=(batch, nt),
        in_specs=[
            pl.BlockSpec((tt, d), lambda b, c: (b * nt + c, 0)),
            _const_spec((1, d)),
            _const_spec(w["q"].shape), _const_spec(w["k"].shape), _const_spec(w["v"].shape),
            _const_spec(w["g"].shape),
            pl.BlockSpec((tt, RET_DK // 2), lambda b, c: (c, 0)),
            pl.BlockSpec((tt, RET_DK // 2), lambda b, c: (c, 0)),
            _const_spec(decay.shape), _const_spec(cross.shape), _const_spec(kdec.shape), _const_spec(gl.shape),
            _const_spec((1, dv)), _const_spec((1, dv)), _const_spec(w["o"].shape),
        ],
        out_specs=[
            pl.BlockSpec((tt, d), lambda b, c: (b * nt + c, 0)),
            pl.BlockSpec((None, nh, RET_DK, RET_DV), lambda b, c: (b, 0, 0, 0)),
        ],
        out_shape=[jax.ShapeDtypeStruct((n, d), F32), jax.ShapeDtypeStruct((batch, nh, RET_DK, RET_DV), F32)],
        scratch_shapes=[pltpu.VMEM((nh, RET_DK, RET_DV), F32), pltpu.VMEM((tt, d), F32)],
        compiler_params=_cparams(("parallel", "arbitrary"), 56),
        name="ret_prompt",
    )(x, gm.reshape(1, d), w["q"], w["k"], w["v"], w["g"], cos, sin, decay, cross, kdec, gl,
      w["gng"].reshape(1, dv), w["gnb"].reshape(1, dv), w["o"])


def _ret_proj_kernel(x_ref, gm_ref, wq_ref, wk_ref, wv_ref, wg_ref, cos_ref, sin_ref, q_ref, k_ref, v_ref, g_ref):
    nh = wq_ref.shape[1] // RET_DK
    q, k, v, g = _ret_project(x_ref[...], gm_ref[...], wq_ref[...], wk_ref[...], wv_ref[...], wg_ref[...],
                              cos_ref[...], sin_ref[...], nh)
    q_ref[...] = q
    k_ref[...] = k
    v_ref[...] = v
    g_ref[...] = g


def _ret_proj(x, gm, w, cos, sin, *, tm):
    n, d = x.shape
    dk = w["q"].shape[1]
    dv = w["v"].shape[1]
    assert n % tm == 0
    tok = lambda width: pl.BlockSpec((tm, width), lambda i: (i, 0))
    return pl.pallas_call(
        _ret_proj_kernel,
        grid=(n // tm,),
        in_specs=[tok(d), _const_spec((1, d)), _const_spec(w["q"].shape), _const_spec(w["k"].shape),
                  _const_spec(w["v"].shape), _const_spec(w["g"].shape), tok(RET_DK // 2), tok(RET_DK // 2)],
        out_specs=[tok(dk), tok(dk), tok(dv), tok(dv)],
        out_shape=[jax.ShapeDtypeStruct((n, dk), F32), jax.ShapeDtypeStruct((n, dk), F32),
                   jax.ShapeDtypeStruct((n, dv), F32), jax.ShapeDtypeStruct((n, dv), F32)],
        compiler_params=_cparams(("parallel",), 48),
        name="ret_proj",
    )(x, gm.reshape(1, d), w["q"], w["k"], w["v"], w["g"], cos, sin)


RET_NEW_PAD = 128


def _ret_sample_kernel(q_ref, k_ref, v_ref, st_ref, decay_ref, cross_ref, kdec_ref, gl_ref, o_ref, ns_ref):
    tp = q_ref.shape[0]
    nh = st_ref.shape[0]
    zpad = RET_NEW_PAD - tp
    for i in range(nh):
        qh = q_ref[:, i * RET_DK:(i + 1) * RET_DK].astype(BF16)
        kh = jnp.concatenate([k_ref[:, i * RET_DK:(i + 1) * RET_DK], jnp.zeros((zpad, RET_DK), F32)], axis=0)
        vh = jnp.concatenate([v_ref[:, i * RET_DV:(i + 1) * RET_DV], jnp.zeros((zpad, RET_DV), F32)],
                             axis=0).astype(BF16)
        kt = kh.T
        state = st_ref[i]
        inner = _mm(qh, kt.astype(BF16)) * decay_ref[i]
        o = _mm(inner.astype(BF16), vh) + _mm(qh, state.astype(BF16)) * cross_ref[i]
        o_ref[:, i * RET_DV:(i + 1) * RET_DV] = o
        ns_ref[i] = state * gl_ref[i] + _mm((kt * kdec_ref[i]).astype(BF16), vh)


def _ret_sample(q, k, v, state, n_new):
    db, tp, dk = q.shape
    dv = v.shape[2]
    nh = state.shape[1]
    log_gamma = jnp.log(1.0 - 2.0 ** (-5.0 - jnp.arange(nh, dtype=F32)))
    n = jnp.arange(tp, dtype=F32)
    m = jnp.arange(RET_NEW_PAD, dtype=F32)
    diff = n[:, None] - m[None, :]
    live = (diff >= 0) & (m[None, :] < n_new)
    decay = jnp.exp(jnp.where(live[None], diff[None] * log_gamma[:, None, None], -jnp.inf))
    cross = jnp.exp((n[None, :] + 1.0) * log_gamma[:, None])[:, :, None]
    kdec = jnp.where(m[None, :] < n_new, jnp.exp((n_new - 1.0 - m)[None, :] * log_gamma[:, None]), 0.0)[:, None, :]
    gl = jnp.broadcast_to(jnp.exp(n_new * log_gamma)[:, None, None], (nh, 1, RET_DV))
    return pl.pallas_call(
        _ret_sample_kernel,
        grid=(db,),
        in_specs=[
            pl.BlockSpec((None, tp, dk), lambda b: (b, 0, 0)),
            pl.BlockSpec((None, tp, dk), lambda b: (b, 0, 0)),
            pl.BlockSpec((None, tp, dv), lambda b: (b, 0, 0)),
            pl.BlockSpec((None, nh, RET_DK, RET_DV), lambda b: (b, 0, 0, 0)),
            _const_spec(decay.shape), _const_spec(cross.shape), _const_spec(kdec.shape), _const_spec(gl.shape),
        ],
        out_specs=[
            pl.BlockSpec((None, tp, dv), lambda b: (b, 0, 0)),
            pl.BlockSpec((None, nh, RET_DK, RET_DV), lambda b: (b, 0, 0, 0)),
        ],
        out_shape=[jax.ShapeDtypeStruct((db, tp, dv), F32), jax.ShapeDtypeStruct(state.shape, F32)],
        compiler_params=_cparams(("parallel",), 32),
        name="ret_sample",
    )(q, k, v, state, decay, cross, kdec, gl)


def _ret_out_kernel(x_ref, o_ref_in, g_ref, gng_ref, gnb_ref, wo_ref, o_ref):
    nh = o_ref_in.shape[1] // RET_DV
    gated = jnp.concatenate(
        [_ret_gate(o_ref_in[:, i * RET_DV:(i + 1) * RET_DV], g_ref[:, i * RET_DV:(i + 1) * RET_DV],
                   gng_ref[:, i * RET_DV:(i + 1) * RET_DV], gnb_ref[:, i * RET_DV:(i + 1) * RET_DV])
         for i in range(nh)], axis=-1)
    o_ref[...] = x_ref[...] + _mm(gated, wo_ref[...])


def _ret_out(x, o, g, gng, gnb, wo, *, tm):
    n, d = x.shape
    dv = o.shape[1]
    assert n % tm == 0
    tok = lambda width: pl.BlockSpec((tm, width), lambda i: (i, 0))
    return pl.pallas_call(
        _ret_out_kernel,
        grid=(n // tm,),
        in_specs=[tok(d), tok(dv), tok(dv), _const_spec((1, dv)), _const_spec((1, dv)), _const_spec(wo.shape)],
        out_specs=tok(d),
        out_shape=jax.ShapeDtypeStruct((n, d), F32),
        compiler_params=_cparams(("parallel",), 48),
        name="ret_out",
    )(x, o, g, gng.reshape(1, dv), gnb.reshape(1, dv), wo)


def _rope_tables(pos, dim):
    if dim == MLA_ROPE:
        inv = ROPE_BASE ** (-jnp.arange(0, dim, 2, dtype=F32) / dim)
    else:
        inv = 1.0 / (ROPE_BASE ** jnp.linspace(0.0, 1.0, dim // 2, dtype=F32))
    ang = pos.astype(F32)[:, None] * inv[None, :]
    return jnp.cos(ang), jnp.sin(ang)


def _mla_tables(pos):
    cos, sin = _rope_tables(pos, MLA_ROPE)
    z = jnp.zeros((pos.shape[0], LANE - MLA_ROPE), F32)
    return jnp.concatenate([cos, cos, z], axis=1), jnp.concatenate([sin, sin, z], axis=1)


def _rot_half_cols(w):
    half = w.shape[-1] // 2
    return jnp.concatenate([-w[..., half:], w[..., :half]], axis=-1)


def _pad_cols(w, width):
    return jnp.pad(w, [(0, 0)] * (w.ndim - 1) + [(0, width - w.shape[-1])])


def _mla_weights(w_dq, g_q, w_uq, w_dkv, g_kv, w_uk, w_uv, w_o):
    ql = w_dq.shape[1]
    uq = w_uq.reshape(ql, MLA_HEADS, MLA_NOPE + MLA_ROPE)
    rope = uq[..., MLA_NOPE:]
    uq = jnp.concatenate([uq[..., :MLA_NOPE], _pad_cols(rope, LANE), _pad_cols(_rot_half_cols(rope), LANE)], axis=-1)
    k_rope = w_dkv[:, MLA_KV_LORA:]
    dkv = jnp.concatenate([w_dkv[:, :MLA_KV_LORA], _pad_cols(k_rope, LANE), _pad_cols(_rot_half_cols(k_rope), LANE)],
                          axis=-1)
    return {
        "dq": w_dq.astype(BF16), "gq": g_q, "uq": uq.reshape(ql, MLA_HEADS * 3 * LANE).astype(BF16),
        "dkv": dkv.astype(BF16), "gkv": g_kv,
        "uk": jnp.transpose(w_uk, (1, 2, 0)).astype(BF16),
        "uv": jnp.transpose(w_uv, (1, 0, 2)).astype(BF16),
        "o": w_o.astype(BF16),
    }


def kernel(x_prompt, x_sample, state_conv, cache_mla_latent, cache_mla_krope, state_ret, page_table, norm_mix, norm_ffn, norm_final, conv_w_pw1, conv_b_pw1, conv_w_dw, conv_b_dw, conv_ln_g, conv_ln_b, conv_w_pw2, conv_b_pw2, mla_w_dq, mla_g_q, mla_w_uq, mla_w_dkv, mla_g_kv, mla_w_uk, mla_w_uv, mla_w_o, ret_w_q, ret_w_k, ret_w_v, ret_w_g, ret_gn_g, ret_gn_b, ret_w_o, ffn_w1, ffn_w3, ffn_w2):
    batch, seq, d = x_prompt.shape
    db, n_new, _ = x_sample.shape
    depth = norm_mix.shape[0]
    past = page_table.shape[1] * PAGE_SIZE
    ns = db * n_new
    tm_p = min(512, seq)
    tm_s = min(256, ns)

    xp = x_prompt.reshape(batch * seq, d)
    xs = jnp.transpose(x_sample, (1, 0, 2)).reshape(ns, d)
    pos_p = jnp.arange(seq)
    pos_s = jnp.repeat(past + jnp.arange(n_new), db)

    def to_t_major(a):
        return jnp.swapaxes(a, 0, 1).reshape(ns, *a.shape[2:])

    def to_b_major(a):
        return jnp.swapaxes(a.reshape(n_new, db, *a.shape[1:]), 0, 1)

    conv_p, conv_s, lat_p, kr_p, lat_s, kr_s, ret_p, ret_s = [], [], [], [], [], [], [], []
    for i in range(depth):
        j = i // N_MIXERS
        kind = i % N_MIXERS
        if kind == 0:
            cw = (norm_mix[i], conv_w_pw1[j].astype(BF16), conv_b_pw1[j], conv_w_dw[j], conv_b_dw[j],
                  conv_ln_g[j], conv_ln_b[j], conv_w_pw2[j].astype(BF16), conv_b_pw2[j])
            xp, st_p = _conv_prompt(xp, seq, *cw, tt=tm_p)
            xs3, st_s = _conv_sample(xs.reshape(n_new, db, d), state_conv[j].reshape(db, CONV_PREV * d), *cw,
                                     bb=min(32, db))
            xs = xs3.reshape(ns, d)
            conv_p.append(st_p)
            conv_s.append(st_s.reshape(db, CONV_PREV, d))
        elif kind == 1:
            w = _mla_weights(mla_w_dq[j], mla_g_q[j], mla_w_uq[j], mla_w_dkv[j], mla_g_kv[j], mla_w_uk[j],
                             mla_w_uv[j], mla_w_o[j])
            q, k, ckv, kr = _mla_proj(xp, norm_mix[i], w, *_mla_tables(pos_p), tm=tm_p)
            a = _mla_attn_prompt(q, k, seq, tq=min(256, seq))
            xp = _mla_out(xp, a, w["uv"], w["o"], tm=tm_p)
            lat_p.append(ckv.reshape(batch, seq, MLA_KV_LORA))
            kr_p.append(kr.reshape(batch, seq, MLA_ROPE))

            q, k, ckv, kr = _mla_proj(xs, norm_mix[i], w, *_mla_tables(pos_s), tm=tm_s)
            q_b = jnp.transpose(q.reshape(MLA_HEADS, n_new, db, MLA_QK), (2, 0, 1, 3)).reshape(
                db, MLA_HEADS * n_new, MLA_QK)
            k_b = jnp.pad(to_b_major(k).astype(F32), ((0, 0), (0, 8 - n_new), (0, 0)))
            a_b = _mla_attn_sample(q_b, k_b, cache_mla_latent[j], cache_mla_krope[j], page_table, n_new)
            a = jnp.transpose(a_b.reshape(db, MLA_HEADS, n_new, MLA_KV_LORA), (1, 2, 0, 3)).reshape(
                MLA_HEADS, ns, MLA_KV_LORA)
            xs = _mla_out(xs, a, w["uv"], w["o"], tm=tm_s)
            lat_s.append(to_b_major(ckv))
            kr_s.append(to_b_major(kr))
        else:
            w = {"q": ret_w_q[j].astype(BF16), "k": ret_w_k[j].astype(BF16), "v": ret_w_v[j].astype(BF16),
                 "g": ret_w_g[j].astype(BF16), "gng": ret_gn_g[j], "gnb": ret_gn_b[j],
                 "o": ret_w_o[j].astype(BF16)}
            xp, st_p = _ret_prompt(xp, seq, norm_mix[i], w, *_rope_tables(pos_p, RET_DK), tt=tm_p,
                                   lc=min(256, seq))
            ret_p.append(st_p)

            q, k, v, g = _ret_proj(xs, norm_mix[i], w, *_rope_tables(pos_s, RET_DK), tm=tm_s)
            pad = lambda a: jnp.pad(to_b_major(a), ((0, 0), (0, 8 - n_new), (0, 0)))
            o_b, st_s = _ret_sample(pad(q), pad(k), pad(v), state_ret[j], n_new)
            xs = _ret_out(xs, to_t_major(o_b[:, :n_new]), g, w["gng"], w["gnb"], w["o"], tm=tm_s)
            ret_s.append(st_s)
        last = i == depth - 1
        fw = (norm_ffn[i], ffn_w1[i].astype(BF16), ffn_w3[i].astype(BF16), ffn_w2[i].astype(BF16),
              norm_final if last else None)
        xp = _ffn(xp, *fw, tm=tm_p)
        xs = _ffn(xs, *fw, tm=tm_s)

    return (xp.reshape(batch, seq, d), to_b_major(xs),
            jnp.stack(conv_p), jnp.stack(conv_s),
            jnp.stack(lat_p), jnp.stack(kr_p), jnp.stack(lat_s), jnp.stack(kr_s),
            jnp.stack(ret_p), jnp.stack(ret_s))
```

```python
import functools
import math

import jax
import jax.numpy as jnp
from jax import lax
from jax.experimental import pallas as pl
from jax.experimental.pallas import tpu as pltpu

F32 = jnp.float32
BF16 = jnp.bfloat16

EPS = 1e-6
NEG_INF = -1e30
ROPE_BASE = 10000.0
PAGE_SIZE = 128
N_MIXERS = 3
CONV_WIDTH = 31
CONV_PREV = CONV_WIDTH - 1
MLA_HEADS = 8
MLA_KV_LORA = 256
MLA_NOPE = 128
MLA_ROPE = 64
MLA_V = 128
MLA_QK = MLA_KV_LORA + 128
MLA_SCALE = (MLA_NOPE + MLA_ROPE) ** -0.5
MLA_QSCALE = MLA_SCALE * math.log2(math.e)
RET_DK = 256
RET_DV = 512

LANE = 128
SUBLANE = 8
MIB = 1024 * 1024


def _cparams(sem, vmem_mib):
    return pltpu.CompilerParams(dimension_semantics=sem, vmem_limit_bytes=vmem_mib * MIB)


def _const_spec(shape):
    nd = len(shape)
    return pl.BlockSpec(shape, lambda *_: (0,) * nd, pipeline_mode=pl.Buffered(1))


def _mm(a, b):
    return jnp.dot(a, b, preferred_element_type=F32)


def _mm_nt(a, b):
    return lax.dot_general(a, b, (((1,), (1,)), ((), ())), preferred_element_type=F32)


def _rms(x, g):
    return x * lax.rsqrt(jnp.mean(x * x, axis=-1, keepdims=True) + EPS) * g


def _silu(x):
    return x * jax.nn.sigmoid(x)


def _ffn_kernel(x_ref, g_ref, w1_ref, w3_ref, w2_ref, *rest, final):
    if final:
        gf_ref, o_ref, h_ref, acc_ref = rest
    else:
        o_ref, h_ref, acc_ref = rest
    k = pl.program_id(1)

    @pl.when(k == 0)
    def _():
        x = x_ref[...]
        h_ref[...] = _rms(x, g_ref[...]).astype(BF16)
        acc_ref[...] = x

    h = h_ref[...]
    a = _mm(h, w1_ref[...])
    b = _mm(h, w3_ref[...])
    acc_ref[...] += _mm((_silu(a) * b).astype(BF16), w2_ref[...])

    @pl.when(k == pl.num_programs(1) - 1)
    def _():
        y = acc_ref[...]
        if final:
            y = _rms(y, gf_ref[...])
        o_ref[...] = y


def _ffn(x, g, w1, w3, w2, g_final=None, *, tm, n_ff_chunks=2):
    n, d = x.shape
    dff = w1.shape[1]
    tf = dff // n_ff_chunks
    assert n % tm == 0 and dff % n_ff_chunks == 0 and tf % LANE == 0
    final = g_final is not None
    in_specs = [
        pl.BlockSpec((tm, d), lambda i, k: (i, 0)),
        _const_spec((1, d)),
        pl.BlockSpec((d, tf), lambda i, k: (0, k)),
        pl.BlockSpec((d, tf), lambda i, k: (0, k)),
        pl.BlockSpec((tf, d), lambda i, k: (k, 0)),
    ]
    args = [x, g.reshape(1, d), w1, w3, w2]
    if final:
        in_specs.append(_const_spec((1, d)))
        args.append(g_final.reshape(1, d))
    return pl.pallas_call(
        functools.partial(_ffn_kernel, final=final),
        grid=(n // tm, n_ff_chunks),
        in_specs=in_specs,
        out_specs=pl.BlockSpec((tm, d), lambda i, k: (i, 0)),
        out_shape=jax.ShapeDtypeStruct((n, d), F32),
        scratch_shapes=[pltpu.VMEM((tm, d), BF16), pltpu.VMEM((tm, d), F32)],
        compiler_params=_cparams(("parallel", "arbitrary"), 48),
        name="ffn_final" if final else "ffn",
    )(*args)


def _conv_glu(x, gm, w1, b1):
    d = x.shape[-1]
    a = _mm(_rms(x, gm).astype(BF16), w1) + b1
    return a[:, :d] * jax.nn.sigmoid(a[:, d:])


def _conv_tail(c, lng, lnb, w2, b2):
    mu = jnp.mean(c, axis=-1, keepdims=True)
    cc = c - mu
    cn = cc * lax.rsqrt(jnp.mean(cc * cc, axis=-1, keepdims=True) + EPS) * lng + lnb
    return _mm(_silu(cn).astype(BF16), w2) + b2


CONV_CARRY = 32
CONV_ROWS = 64
CONV_COLS = 256


def _conv_prompt_kernel(x_ref, gm_ref, w1_ref, b1_ref, wdw_ref, bdw_ref, lng_ref, lnb_ref, w2_ref, b2_ref,
                        o_ref, st_ref, ubuf_ref, c_ref):
    tt, d = x_ref.shape
    t = pl.program_id(1)
    tail = tt + CONV_CARRY - SUBLANE

    @pl.when(t == 0)
    def _():
        for s in range(SUBLANE):
            ubuf_ref[s, 0:CONV_CARRY, :] = jnp.zeros((CONV_CARRY, d), F32)
            ubuf_ref[s, tail:tail + SUBLANE, :] = jnp.zeros((SUBLANE, d), F32)

    x = x_ref[...]
    u = _conv_glu(x, gm_ref[...], w1_ref[...], b1_ref[...])
    for s in range(SUBLANE):
        ubuf_ref[s, CONV_CARRY - s:CONV_CARRY - s + tt, :] = u

    lead = CONV_CARRY - CONV_PREV
    for r0 in range(0, tt, CONV_ROWS):
        for c0 in range(0, d, CONV_COLS):
            cols = slice(c0, c0 + CONV_COLS)
            acc = jnp.broadcast_to(bdw_ref[:, cols], (CONV_ROWS, CONV_COLS))
            for k in range(CONV_WIDTH):
                s = (lead + k) % SUBLANE
                row = r0 + lead + k - s
                acc = acc + ubuf_ref[s, row:row + CONV_ROWS, cols] * wdw_ref[k:k + 1, cols]
            c_ref[r0:r0 + CONV_ROWS, cols] = acc

    y = _conv_tail(c_ref[...], lng_ref[...], lnb_ref[...], w2_ref[...], b2_ref[...])
    o_ref[...] = x + y
    st_ref[...] = ubuf_ref[0, tt + lead:tt + CONV_CARRY, :]
    for s in range(SUBLANE):
        ubuf_ref[s, 0:CONV_CARRY, :] = ubuf_ref[s, tt:tt + CONV_CARRY, :]


def _conv_prompt(x, seq, gm, w1, b1, wdw, bdw, lng, lnb, w2, b2, *, tt):
    n, d = x.shape
    batch = n // seq
    nt = seq // tt
    assert seq % tt == 0 and tt % CONV_ROWS == 0 and d % CONV_COLS == 0
    row = lambda a: a.reshape(1, -1)
    return pl.pallas_call(
        _conv_prompt_kernel,
        grid=(batch, nt),
        in_specs=[
            pl.BlockSpec((tt, d), lambda b, t: (b * nt + t, 0)),
            _const_spec((1, d)), _const_spec((d, 2 * d)), _const_spec((1, 2 * d)),
            _const_spec((CONV_WIDTH, d)), _const_spec((1, d)), _const_spec((1, d)), _const_spec((1, d)),
            _const_spec((d, d)), _const_spec((1, d)),
        ],
        out_specs=[
            pl.BlockSpec((tt, d), lambda b, t: (b * nt + t, 0)),
            pl.BlockSpec((None, CONV_PREV, d), lambda b, t: (b, 0, 0)),
        ],
        out_shape=[jax.ShapeDtypeStruct((n, d), F32), jax.ShapeDtypeStruct((batch, CONV_PREV, d), F32)],
        scratch_shapes=[pltpu.VMEM((SUBLANE, tt + CONV_CARRY, d), F32), pltpu.VMEM((tt, d), F32)],
        compiler_params=_cparams(("parallel", "arbitrary"), 56),
        name="conv_prompt",
    )(x, row(gm), w1, row(b1), wdw, row(bdw), row(lng), row(lnb), w2, row(b2))


def _conv_sample_kernel(x_ref, st_ref, gm_ref, w1_ref, b1_ref, wdw_ref, bdw_ref, lng_ref, lnb_ref, w2_ref, b2_ref,
                        o_ref, ns_ref):
    nt, bb, d = x_ref.shape
    x = x_ref[...].reshape(nt * bb, d)
    u = _conv_glu(x, gm_ref[...], w1_ref[...], b1_ref[...])
    us = [u[i * bb:(i + 1) * bb] for i in range(nt)]

    def buf(j):
        return st_ref[j] if j < CONV_PREV else us[j - CONV_PREV]

    cs = []
    for t in range(nt):
        acc = jnp.broadcast_to(bdw_ref[...], (bb, d))
        for k in range(CONV_WIDTH):
            acc = acc + buf(t + k) * wdw_ref[k:k + 1, :]
        cs.append(acc)
    c = jnp.concatenate(cs, axis=0)
    y = _conv_tail(c, lng_ref[...], lnb_ref[...], w2_ref[...], b2_ref[...])
    o_ref[...] = (x + y).reshape(nt, bb, d)
    keep = CONV_PREV - nt
    ns_ref[0:keep] = st_ref[nt:CONV_PREV]
    for i in range(nt):
        ns_ref[keep + i] = us[i]


def _conv_sample(x3, state, gm, w1, b1, wdw, bdw, lng, lnb, w2, b2, *, bb):
    nt, db, d = x3.shape
    assert db % bb == 0 and nt <= CONV_PREV
    row = lambda a: a.reshape(1, -1)
    return pl.pallas_call(
        _conv_sample_kernel,
        grid=(db // bb,),
        in_specs=[
            pl.BlockSpec((nt, bb, d), lambda i: (0, i, 0)),
            pl.BlockSpec((CONV_PREV, bb, d), lambda i: (0, i, 0)),
            _const_spec((1, d)), _const_spec((d, 2 * d)), _const_spec((1, 2 * d)),
            _const_spec((CONV_WIDTH, d)), _const_spec((1, d)), _const_spec((1, d)), _const_spec((1, d)),
            _const_spec((d, d)), _const_spec((1, d)),
        ],
        out_specs=[
            pl.BlockSpec((nt, bb, d), lambda i: (0, i, 0)),
            pl.BlockSpec((CONV_PREV, bb, d), lambda i: (0, i, 0)),
        ],
        out_shape=[jax.ShapeDtypeStruct((nt, db, d), F32), jax.ShapeDtypeStruct((CONV_PREV, db, d), F32)],
        compiler_params=_cparams(("parallel",), 48),
        name="conv_sample",
    )(x3, state, row(gm), w1, row(b1), wdw, row(bdw), row(lng), row(lnb), w2, row(b2))


def _mla_proj_kernel(x_ref, gm_ref, wdq_ref, gq_ref, wuq_ref, wdkv_ref, gkv_ref, wuk_ref, cos_ref, sin_ref,
                     *rest, transposed):
    if transposed:
        cos_t_ref, sin_t_ref, q_ref, k_ref, kt_ref, ckv_ref, kr_ref = rest
    else:
        q_ref, k_ref, ckv_ref, kr_ref = rest
    h = _rms(x_ref[...], gm_ref[...]).astype(BF16)
    cos = cos_ref[...]
    sin = sin_ref[...]
    cq = _rms(_mm(h, wdq_ref[...]), gq_ref[...])
    nh = q_ref.shape[0]
    if transposed:
        qt = _mm(wuq_ref[...], cq.T.astype(BF16))
        cos_t = cos_t_ref[...]
        sin_t = sin_t_ref[...]
        for i in range(nh):
            base = i * 3 * LANE
            q_lat = _mm(wuk_ref[i], qt[base:base + LANE].astype(BF16))
            q_rot = qt[base + LANE:base + 2 * LANE] * cos_t + qt[base + 2 * LANE:base + 3 * LANE] * sin_t
            q_ref[i, 0:MLA_KV_LORA, :] = (q_lat * MLA_QSCALE).astype(BF16)
            q_ref[i, MLA_KV_LORA:MLA_QK, :] = (q_rot * MLA_QSCALE).astype(BF16)
    else:
        q = _mm(cq.astype(BF16), wuq_ref[...])
        for i in range(nh):
            base = i * 3 * LANE
            q_lat = _mm(q[:, base:base + LANE].astype(BF16), wuk_ref[i])
            q_rot = q[:, base + LANE:base + 2 * LANE] * cos + q[:, base + 2 * LANE:base + 3 * LANE] * sin
            q_ref[i, :, 0:MLA_KV_LORA] = (q_lat * MLA_QSCALE).astype(BF16)
            q_ref[i, :, MLA_KV_LORA:MLA_QK] = (q_rot * MLA_QSCALE).astype(BF16)
    a = _mm(h, wdkv_ref[...])
    ckv = _rms(a[:, 0:MLA_KV_LORA], gkv_ref[...])
    k_rot = a[:, MLA_KV_LORA:MLA_QK] * cos + a[:, MLA_QK:MLA_QK + LANE] * sin
    ckv_ref[...] = ckv
    kr_ref[...] = k_rot[:, 0:MLA_ROPE]
    k_ref[:, 0:MLA_KV_LORA] = ckv.astype(BF16)
    k_ref[:, MLA_KV_LORA:MLA_QK] = k_rot.astype(BF16)
    if transposed:
        kt_ref[...] = jnp.concatenate([ckv, k_rot], axis=1).T.astype(BF16)


def _mla_proj(x, gm, w, cos, sin, *, tm, transposed):
    n, d = x.shape
    period = cos.shape[0] // tm
    assert n % tm == 0 and cos.shape[0] % tm == 0
    ql = w["dq"].shape[1]
    wuq, wuk = (w["uq_t"], w["uk_t"]) if transposed else (w["uq"], w["uk"])
    tok = lambda width: pl.BlockSpec((tm, width), lambda i: (i, 0))
    in_specs = [
        tok(d), _const_spec((1, d)), _const_spec((d, ql)), _const_spec((1, ql)),
        _const_spec(wuq.shape), _const_spec(w["dkv"].shape), _const_spec((1, MLA_KV_LORA)), _const_spec(wuk.shape),
        pl.BlockSpec((tm, LANE), lambda i: (i % period, 0)),
        pl.BlockSpec((tm, LANE), lambda i: (i % period, 0)),
    ]
    args = [x, gm.reshape(1, d), w["dq"], w["gq"].reshape(1, ql), wuq, w["dkv"], w["gkv"].reshape(1, MLA_KV_LORA),
            wuk, cos, sin]
    out_specs = [tok(MLA_QK), tok(MLA_KV_LORA), tok(MLA_ROPE)]
    out_shape = [jax.ShapeDtypeStruct((n, MLA_QK), BF16), jax.ShapeDtypeStruct((n, MLA_KV_LORA), F32),
                 jax.ShapeDtypeStruct((n, MLA_ROPE), F32)]
    if transposed:
        in_specs += [pl.BlockSpec((LANE, tm), lambda i: (0, i % period))] * 2
        args += [cos.T, sin.T]
        out_specs = ([pl.BlockSpec((MLA_HEADS, MLA_QK, tm), lambda i: (0, 0, i)), out_specs[0],
                      pl.BlockSpec((MLA_QK, tm), lambda i: (0, i))] + out_specs[1:])
        out_shape = ([jax.ShapeDtypeStruct((MLA_HEADS, MLA_QK, n), BF16), out_shape[0],
                      jax.ShapeDtypeStruct((MLA_QK, n), BF16)] + out_shape[1:])
    else:
        out_specs = [pl.BlockSpec((MLA_HEADS, tm, MLA_QK), lambda i: (0, i, 0))] + out_specs
        out_shape = [jax.ShapeDtypeStruct((MLA_HEADS, n, MLA_QK), BF16)] + out_shape
    return pl.pallas_call(
        functools.partial(_mla_proj_kernel, transposed=transposed),
        grid=(n // tm,),
        in_specs=in_specs,
        out_specs=out_specs,
        out_shape=out_shape,
        compiler_params=_cparams(("parallel",), 48),
        name="mla_proj_t" if transposed else "mla_proj",
    )(*args)


def _mla_attn_prompt_kernel(qi_ref, kj_ref, qt_ref, k_ref, vt_ref, o_ref, m_ref, l_ref, acc_ref):
    nh, dq, tq = qt_ref.shape
    tk = k_ref.shape[0]
    p = pl.program_id(1)
    i = qi_ref[p]
    j = kj_ref[p]

    @pl.when(j == 0)
    def _():
        m_ref[...] = jnp.full(m_ref.shape, NEG_INF, F32)
        l_ref[...] = jnp.zeros(l_ref.shape, F32)
        acc_ref[...] = jnp.zeros(acc_ref.shape, F32)

    def update(h, diagonal):
        s = _mm(k_ref[...], qt_ref[h])
        if diagonal:
            causal = lax.broadcasted_iota(jnp.int32, (tk, tq), 0) <= lax.broadcasted_iota(jnp.int32, (tk, tq), 1)
            s = jnp.where(causal, s, NEG_INF)
        m_prev = m_ref[h]
        m_new = jnp.maximum(m_prev, jnp.max(s, axis=0, keepdims=True))
        alpha = jnp.exp2(m_prev - m_new)
        e = jnp.exp2(s - m_new)
        l_new = alpha * l_ref[h] + jnp.sum(e, axis=0, keepdims=True)
        acc = alpha * acc_ref[h] + _mm(vt_ref[...], e.astype(BF16))
        if diagonal:
            o_ref[h] = (acc / l_new).T.astype(o_ref.dtype)
        else:
            m_ref[h] = m_new
            l_ref[h] = l_new
            acc_ref[h] = acc

    @pl.when(j < i)
    def _():
        for h in range(nh):
            update(h, False)

    @pl.when(j == i)
    def _():
        for h in range(nh):
            update(h, True)


def _mla_attn_prompt(qt, k, kt, seq, *, tq):
    nh, dq, n = qt.shape
    batch = n // seq
    nb = seq // tq
    assert seq % tq == 0
    pairs = [(i, j) for i in range(nb) for j in range(i + 1)]
    qi = jnp.asarray([p[0] for p in pairs], jnp.int32)
    kj = jnp.asarray([p[1] for p in pairs], jnp.int32)
    grid_spec = pltpu.PrefetchScalarGridSpec(
        num_scalar_prefetch=2,
        grid=(batch, len(pairs)),
        in_specs=[
            pl.BlockSpec((nh, dq, tq), lambda b, p, qi, kj: (0, 0, b * nb + qi[p])),
            pl.BlockSpec((tq, dq), lambda b, p, qi, kj: (b * nb + kj[p], 0)),
            pl.BlockSpec((MLA_KV_LORA, tq), lambda b, p, qi, kj: (0, b * nb + kj[p])),
        ],
        out_specs=pl.BlockSpec((nh, tq, MLA_KV_LORA), lambda b, p, qi, kj: (0, b * nb + qi[p], 0)),
        scratch_shapes=[pltpu.VMEM((nh, 1, tq), F32), pltpu.VMEM((nh, 1, tq), F32),
                        pltpu.VMEM((nh, MLA_KV_LORA, tq), F32)],
    )
    return pl.pallas_call(
        _mla_attn_prompt_kernel,
        grid_spec=grid_spec,
        out_shape=jax.ShapeDtypeStruct((nh, n, MLA_KV_LORA), BF16),
        compiler_params=_cparams(("parallel", "arbitrary"), 48),
        name="mla_attn_prompt",
    )(qi, kj, qt, k, kt)


MLA_SAMPLE_CHUNK = 1024
MLA_NEW_PAD = 128


def _mla_attn_sample_kernel(pt_ref, q_ref, kn_ref, lat_hbm, kr_hbm, o_ref, lat_buf, kr_buf, s_ref, sem, *, n_new):
    b = pl.program_id(0)
    nb = pl.num_programs(0)
    past = lat_buf.shape[1]
    n_pages = past // PAGE_SIZE
    rows = q_ref.shape[0]
    slot = b % 2

    def page_copies(bi, sl, pg):
        page = pt_ref[bi * n_pages + pg]
        dst = pl.ds(pg * PAGE_SIZE, PAGE_SIZE)
        return (pltpu.make_async_copy(lat_hbm.at[page], lat_buf.at[sl, dst], sem.at[0, sl]),
                pltpu.make_async_copy(kr_hbm.at[page], kr_buf.at[sl, :, dst], sem.at[1, sl]))

    def start_fetch(bi, sl):
        def body(pg, _):
            for c in page_copies(bi, sl, pg):
                c.start()
            return 0
        lax.fori_loop(0, n_pages, body, 0)

    @pl.when(b == 0)
    def _():
        start_fetch(0, 0)

    @pl.when(b + 1 < nb)
    def _():
        start_fetch(b + 1, 1 - slot)

    def wait_body(pg, _):
        for c in page_copies(b, slot, pg):
            c.wait()
        return 0
    lax.fori_loop(0, n_pages, wait_body, 0)

    q = q_ref[...]
    q_lat = q[:, 0:MLA_KV_LORA]
    q_rope = q[:, MLA_KV_LORA:MLA_KV_LORA + MLA_ROPE]
    ch = MLA_SAMPLE_CHUNK
    n_chunks = past // ch

    def score_body(c, m):
        off = pl.multiple_of(c * ch, ch)
        s = (_mm_nt(q_lat, lat_buf[slot, pl.ds(off, ch), :].astype(BF16))
             + _mm(q_rope, kr_buf[slot, :, pl.ds(off, ch)].astype(BF16)))
        s_ref[:, pl.ds(off, ch)] = s
        return jnp.maximum(m, jnp.max(s, axis=-1, keepdims=True))

    m = lax.fori_loop(0, n_chunks, score_body, jnp.full((rows, 1), NEG_INF, F32))

    kn = kn_ref[...]
    k_new = jnp.concatenate([kn, jnp.zeros((MLA_NEW_PAD - kn.shape[0], kn.shape[1]), F32)], axis=0).astype(BF16)
    s_new = _mm_nt(q, k_new)
    r_tok = lax.broadcasted_iota(jnp.int32, (rows, MLA_NEW_PAD), 0) % n_new
    s_new = jnp.where(lax.broadcasted_iota(jnp.int32, (rows, MLA_NEW_PAD), 1) <= r_tok, s_new, NEG_INF)
    m = jnp.maximum(m, jnp.max(s_new, axis=-1, keepdims=True))
    e_new = jnp.exp2(s_new - m)
    l0 = jnp.sum(e_new, axis=-1, keepdims=True)
    acc0 = _mm(e_new.astype(BF16), k_new[:, 0:MLA_KV_LORA])

    def out_body(c, carry):
        l, acc = carry
        off = pl.multiple_of(c * ch, ch)
        e = jnp.exp2(s_ref[:, pl.ds(off, ch)] - m)
        acc = acc + _mm(e.astype(BF16), lat_buf[slot, pl.ds(off, ch), :].astype(BF16))
        return l + jnp.sum(e, axis=-1, keepdims=True), acc

    l, acc = lax.fori_loop(0, n_chunks, out_body, (l0, acc0))
    o_ref[...] = (acc / l).astype(o_ref.dtype)


def _mla_attn_sample(q, k_new, cache_lat, cache_kr, page_table, n_new):
    db, rows, dq = q.shape
    tp = k_new.shape[1]
    n_pages = page_table.shape[1]
    past = n_pages * PAGE_SIZE
    assert past % MLA_SAMPLE_CHUNK == 0
    grid_spec = pltpu.PrefetchScalarGridSpec(
        num_scalar_prefetch=1,
        grid=(db,),
        in_specs=[
            pl.BlockSpec((None, rows, dq), lambda b, pt: (b, 0, 0)),
            pl.BlockSpec((None, tp, dq), lambda b, pt: (b, 0, 0)),
            pl.BlockSpec(memory_space=pl.ANY),
            pl.BlockSpec(memory_space=pl.ANY),
        ],
        out_specs=pl.BlockSpec((None, rows, MLA_KV_LORA), lambda b, pt: (b, 0, 0)),
        scratch_shapes=[
            pltpu.VMEM((2, past, MLA_KV_LORA), F32),
            pltpu.VMEM((2, MLA_ROPE, past), F32),
            pltpu.VMEM((rows, past), F32),
            pltpu.SemaphoreType.DMA((2, 2)),
        ],
    )
    return pl.pallas_call(
        functools.partial(_mla_attn_sample_kernel, n_new=n_new),
        grid_spec=grid_spec,
        out_shape=jax.ShapeDtypeStruct((db, rows, MLA_KV_LORA), BF16),
        compiler_params=_cparams(("arbitrary",), 48),
        name="mla_attn_sample",
    )(page_table.reshape(-1), q, k_new, cache_lat, cache_kr)


def _mla_out_kernel(x_ref, a_ref, wuv_ref, wo_ref, o_ref):
    nh = a_ref.shape[0]
    o = jnp.concatenate([_mm(a_ref[i], wuv_ref[i]).astype(BF16) for i in range(nh)], axis=-1)
    o_ref[...] = x_ref[...] + _mm(o, wo_ref[...])


def _mla_out(x, a, wuv, wo, *, tm):
    n, d = x.shape
    assert n % tm == 0
    return pl.pallas_call(
        _mla_out_kernel,
        grid=(n // tm,),
        in_specs=[
            pl.BlockSpec((tm, d), lambda i: (i, 0)),
            pl.BlockSpec((MLA_HEADS, tm, MLA_KV_LORA), lambda i: (0, i, 0)),
            _const_spec(wuv.shape), _const_spec(wo.shape),
        ],
        out_specs=pl.BlockSpec((tm, d), lambda i: (i, 0)),
        out_shape=jax.ShapeDtypeStruct((n, d), F32),
        compiler_params=_cparams(("parallel",), 48),
        name="mla_out",
    )(x, a, wuv, wo)


def _ret_project(x, gm, wq, wk, wv, wg, cos, sin, n_heads):
    h = _rms(x, gm).astype(BF16)
    half = RET_DK // 2

    def rot(a):
        outs = []
        for i in range(n_heads):
            a1 = a[:, i * RET_DK:i * RET_DK + half]
            a2 = a[:, i * RET_DK + half:(i + 1) * RET_DK]
            outs += [a1 * cos - a2 * sin, a2 * cos + a1 * sin]
        return jnp.concatenate(outs, axis=-1)

    return rot(_mm(h, wq)), rot(_mm(h, wk)) * (RET_DK ** -0.5), _mm(h, wv), _mm(h, wg)


def _ret_gate(o, g, gng, gnb):
    mu = jnp.mean(o, axis=-1, keepdims=True)
    oc = o - mu
    on = oc * lax.rsqrt(jnp.mean(oc * oc, axis=-1, keepdims=True) + EPS)
    return (_silu(g) * (on * gng + gnb)).astype(BF16)


def _ret_prompt_kernel(x_ref, gm_ref, wq_ref, wk_ref, wv_ref, wg_ref, cos_ref, sin_ref,
                       decay_ref, cross_ref, kdec_ref, gl_ref, gng_ref, gnb_ref, wo_ref,
                       o_ref, st_ref, state_ref, y_ref):
    tt, d = x_ref.shape
    nh, lc, _ = decay_ref.shape
    c = pl.program_id(1)

    @pl.when(c == 0)
    def _():
        state_ref[...] = jnp.zeros(state_ref.shape, F32)

    x = x_ref[...]
    q, k, v, g = _ret_project(x, gm_ref[...], wq_ref[...], wk_ref[...], wv_ref[...], wg_ref[...],
                              cos_ref[...], sin_ref[...], nh)
    y_ref[...] = x
    for r0 in range(0, tt, lc):
        rows = slice(r0, r0 + lc)
        for i in range(nh):
            qh = q[rows, i * RET_DK:(i + 1) * RET_DK].astype(BF16)
            kh = k[rows, i * RET_DK:(i + 1) * RET_DK]
            vh = v[rows, i * RET_DV:(i + 1) * RET_DV].astype(BF16)
            state = state_ref[i]
            inner = _mm_nt(qh, kh.astype(BF16)) * decay_ref[i]
            o = _mm(inner.astype(BF16), vh) + _mm(qh, state.astype(BF16)) * cross_ref[i]
            k_dec = (kh * kdec_ref[i]).T.astype(BF16)
            state_ref[i] = state * gl_ref[i] + _mm(k_dec, vh)
            hv = slice(i * RET_DV, (i + 1) * RET_DV)
            gated = _ret_gate(o, g[rows, hv], gng_ref[:, hv], gnb_ref[:, hv])
            y_ref[rows, :] += _mm(gated, wo_ref[hv, :])
    o_ref[...] = y_ref[...]

    @pl.when(c == pl.num_programs(1) - 1)
    def _():
        st_ref[...] = state_ref[...]


def _ret_tables(n_heads, lc):
    log_gamma = jnp.log(1.0 - 2.0 ** (-5.0 - jnp.arange(n_heads, dtype=F32)))
    n = jnp.arange(lc, dtype=F32)
    diff = n[:, None] - n[None, :]
    decay = jnp.exp(jnp.where(diff[None] >= 0, diff[None] * log_gamma[:, None, None], -jnp.inf))
    cross = jnp.exp((n[None, :] + 1.0) * log_gamma[:, None])[:, :, None]
    kdec = jnp.exp((lc - 1.0 - n)[None, :] * log_gamma[:, None])[:, :, None]
    gl = jnp.exp(lc * log_gamma)
    return decay, cross, kdec, gl


def _ret_prompt(x, seq, gm, w, cos, sin, *, tt, lc):
    n, d = x.shape
    batch = n // seq
    nt = seq // tt
    nh = w["q"].shape[1] // RET_DK
    assert seq % tt == 0 and tt % lc == 0
    decay, cross, kdec, gl = _ret_tables(nh, lc)
    gl = jnp.broadcast_to(gl[:, None, None], (nh, 1, RET_DV))
    dv = nh * RET_DV
    return pl.pallas_call(
        _ret_prompt_kernel,
        grid=(batch, nt),
        in_specs=[
            pl.BlockSpec((tt, d), lambda b, c: (b * nt + c, 0)),
            _const_spec((1, d)),
            _const_spec(w["q"].shape), _const_spec(w["k"].shape), _const_spec(w["v"].shape),
            _const_spec(w["g"].shape),
            pl.BlockSpec((tt, RET_DK // 2), lambda b, c: (c, 0)),
            pl.BlockSpec((tt, RET_DK // 2), lambda b, c: (c, 0)),
            _const_spec(decay.shape), _const_spec(cross.shape), _const_spec(kdec.shape), _const_spec(gl.shape),
            _const_spec((1, dv)), _const_spec((1, dv)), _const_spec(w["o"].shape),
        ],
        out_specs=[
            pl.BlockSpec((tt, d), lambda b, c: (b * nt + c, 0)),
            pl.BlockSpec((None, nh, RET_DK, RET_DV), lambda b, c: (b, 0, 0, 0)),
        ],
        out_shape=[jax.ShapeDtypeStruct((n, d), F32), jax.ShapeDtypeStruct((batch, nh, RET_DK, RET_DV), F32)],
        scratch_shapes=[pltpu.VMEM((nh, RET_DK, RET_DV), F32), pltpu.VMEM((tt, d), F32)],
        compiler_params=_cparams(("parallel", "arbitrary"), 56),
        name="ret_prompt",
    )(x, gm.reshape(1, d), w["q"], w["k"], w["v"], w["g"], cos, sin, decay, cross, kdec, gl,
      w["gng"].reshape(1, dv), w["gnb"].reshape(1, dv), w["o"])


def _ret_proj_kernel(x_ref, gm_ref, wq_ref, wk_ref, wv_ref, wg_ref, cos_ref, sin_ref, q_ref, k_ref, v_ref, g_ref):
    nh = wq_ref.shape[1] // RET_DK
    q, k, v, g = _ret_project(x_ref[...], gm_ref[...], wq_ref[...], wk_ref[...], wv_ref[...], wg_ref[...],
                              cos_ref[...], sin_ref[...], nh)
    q_ref[...] = q
    k_ref[...] = k
    v_ref[...] = v
    g_ref[...] = g


def _ret_proj(x, gm, w, cos, sin, *, tm):
    n, d = x.shape
    dk = w["q"].shape[1]
    dv = w["v"].shape[1]
    assert n % tm == 0
    tok = lambda width: pl.BlockSpec((tm, width), lambda i: (i, 0))
    return pl.pallas_call(
        _ret_proj_kernel,
        grid=(n // tm,),
        in_specs=[tok(d), _const_spec((1, d)), _const_spec(w["q"].shape), _const_spec(w["k"].shape),
                  _const_spec(w["v"].shape), _const_spec(w["g"].shape), tok(RET_DK // 2), tok(RET_DK // 2)],
        out_specs=[tok(dk), tok(dk), tok(dv), tok(dv)],
        out_shape=[jax.ShapeDtypeStruct((n, dk), F32), jax.ShapeDtypeStruct((n, dk), F32),
                   jax.ShapeDtypeStruct((n, dv), F32), jax.ShapeDtypeStruct((n, dv), F32)],
        compiler_params=_cparams(("parallel",), 48),
        name="ret_proj",
    )(x, gm.reshape(1, d), w["q"], w["k"], w["v"], w["g"], cos, sin)


RET_NEW_PAD = 128


def _ret_sample_kernel(q_ref, k_ref, v_ref, st_ref, decay_ref, cross_ref, kdec_ref, gl_ref, o_ref, ns_ref):
    tp = q_ref.shape[0]
    nh = st_ref.shape[0]
    zpad = RET_NEW_PAD - tp
    for i in range(nh):
        qh = q_ref[:, i * RET_DK:(i + 1) * RET_DK].astype(BF16)
        kh = jnp.concatenate([k_ref[:, i * RET_DK:(i + 1) * RET_DK], jnp.zeros((zpad, RET_DK), F32)], axis=0)
        vh = jnp.concatenate([v_ref[:, i * RET_DV:(i + 1) * RET_DV], jnp.zeros((zpad, RET_DV), F32)],
                             axis=0).astype(BF16)
        kt = kh.T
        state = st_ref[i]
        inner = _mm(qh, kt.astype(BF16)) * decay_ref[i]
        o = _mm(inner.astype(BF16), vh) + _mm(qh, state.astype(BF16)) * cross_ref[i]
        o_ref[:, i * RET_DV:(i + 1) * RET_DV] = o
        ns_ref[i] = state * gl_ref[i] + _mm((kt * kdec_ref[i]).astype(BF16), vh)


def _ret_sample(q, k, v, state, n_new):
    db, tp, dk = q.shape
    dv = v.shape[2]
    nh = state.shape[1]
    log_gamma = jnp.log(1.0 - 2.0 ** (-5.0 - jnp.arange(nh, dtype=F32)))
    n = jnp.arange(tp, dtype=F32)
    m = jnp.arange(RET_NEW_PAD, dtype=F32)
    diff = n[:, None] - m[None, :]
    live = (diff >= 0) & (m[None, :] < n_new)
    decay = jnp.exp(jnp.where(live[None], diff[None] * log_gamma[:, None, None], -jnp.inf))
    cross = jnp.exp((n[None, :] + 1.0) * log_gamma[:, None])[:, :, None]
    kdec = jnp.where(m[None, :] < n_new, jnp.exp((n_new - 1.0 - m)[None, :] * log_gamma[:, None]), 0.0)[:, None, :]
    gl = jnp.broadcast_to(jnp.exp(n_new * log_gamma)[:, None, None], (nh, 1, RET_DV))
    return pl.pallas_call(
        _ret_sample_kernel,
        grid=(db,),
        in_specs=[
            pl.BlockSpec((None, tp, dk), lambda b: (b, 0, 0)),
            pl.BlockSpec((None, tp, dk), lambda b: (b, 0, 0)),
            pl.BlockSpec((None, tp, dv), lambda b: (b, 0, 0)),
            pl.BlockSpec((None, nh, RET_DK, RET_DV), lambda b: (b, 0, 0, 0)),
            _const_spec(decay.shape), _const_spec(cross.shape), _const_spec(kdec.shape), _const_spec(gl.shape),
        ],
        out_specs=[
            pl.BlockSpec((None, tp, dv), lambda b: (b, 0, 0)),
            pl.BlockSpec((None, nh, RET_DK, RET_DV), lambda b: (b, 0, 0, 0)),
        ],
        out_shape=[jax.ShapeDtypeStruct((db, tp, dv), F32), jax.ShapeDtypeStruct(state.shape, F32)],
        compiler_params=_cparams(("parallel",), 32),
        name="ret_sample",
    )(q, k, v, state, decay, cross, kdec, gl)


def _ret_out_kernel(x_ref, o_ref_in, g_ref, gng_ref, gnb_ref, wo_ref, o_ref):
    nh = o_ref_in.shape[1] // RET_DV
    gated = jnp.concatenate(
        [_ret_gate(o_ref_in[:, i * RET_DV:(i + 1) * RET_DV], g_ref[:, i * RET_DV:(i + 1) * RET_DV],
                   gng_ref[:, i * RET_DV:(i + 1) * RET_DV], gnb_ref[:, i * RET_DV:(i + 1) * RET_DV])
         for i in range(nh)], axis=-1)
    o_ref[...] = x_ref[...] + _mm(gated, wo_ref[...])


def _ret_out(x, o, g, gng, gnb, wo, *, tm):
    n, d = x.shape
    dv = o.shape[1]
    assert n % tm == 0
    tok = lambda width: pl.BlockSpec((tm, width), lambda i: (i, 0))
    return pl.pallas_call(
        _ret_out_kernel,
        grid=(n // tm,),
        in_specs=[tok(d), tok(dv), tok(dv), _const_spec((1, dv)), _const_spec((1, dv)), _const_spec(wo.shape)],
        out_specs=tok(d),
        out_shape=jax.ShapeDtypeStruct((n, d), F32),
        compiler_params=_cparams(("parallel",), 48),
        name="ret_out",
    )(x, o, g, gng.reshape(1, dv), gnb.reshape(1, dv), wo)


def _rope_tables(pos, dim):
    if dim == MLA_ROPE:
        inv = ROPE_BASE ** (-jnp.arange(0, dim, 2, dtype=F32) / dim)
    else:
        inv = 1.0 / (ROPE_BASE ** jnp.linspace(0.0, 1.0, dim // 2, dtype=F32))
    ang = pos.astype(F32)[:, None] * inv[None, :]
    return jnp.cos(ang), jnp.sin(ang)


def _mla_tables(pos):
    cos, sin = _rope_tables(pos, MLA_ROPE)
    z = jnp.zeros((pos.shape[0], LANE - MLA_ROPE), F32)
    return jnp.concatenate([cos, cos, z], axis=1), jnp.concatenate([sin, sin, z], axis=1)


def _rot_half_cols(w):
    half = w.shape[-1] // 2
    return jnp.concatenate([-w[..., half:], w[..., :half]], axis=-1)


def _pad_cols(w, width):
    return jnp.pad(w, [(0, 0)] * (w.ndim - 1) + [(0, width - w.shape[-1])])


def _mla_weights(w_dq, g_q, w_uq, w_dkv, g_kv, w_uk, w_uv, w_o):
    ql = w_dq.shape[1]
    uq = w_uq.reshape(ql, MLA_HEADS, MLA_NOPE + MLA_ROPE)
    rope = uq[..., MLA_NOPE:]
    uq = jnp.concatenate([uq[..., :MLA_NOPE], _pad_cols(rope, LANE), _pad_cols(_rot_half_cols(rope), LANE)], axis=-1)
    k_rope = w_dkv[:, MLA_KV_LORA:]
    dkv = jnp.concatenate([w_dkv[:, :MLA_KV_LORA], _pad_cols(k_rope, LANE), _pad_cols(_rot_half_cols(k_rope), LANE)],
                          axis=-1)
    uq = uq.reshape(ql, MLA_HEADS * 3 * LANE).astype(BF16)
    return {
        "dq": w_dq.astype(BF16), "gq": g_q, "uq": uq, "uq_t": uq.T,
        "dkv": dkv.astype(BF16), "gkv": g_kv,
        "uk": jnp.transpose(w_uk, (1, 2, 0)).astype(BF16),
        "uk_t": jnp.transpose(w_uk, (1, 0, 2)).astype(BF16),
        "uv": jnp.transpose(w_uv, (1, 0, 2)).astype(BF16),
        "o": w_o.astype(BF16),
    }


def kernel(x_prompt, x_sample, state_conv, cache_mla_latent, cache_mla_krope, state_ret, page_table, norm_mix, norm_ffn, norm_final, conv_w_pw1, conv_b_pw1, conv_w_dw, conv_b_dw, conv_ln_g, conv_ln_b, conv_w_pw2, conv_b_pw2, mla_w_dq, mla_g_q, mla_w_uq, mla_w_dkv, mla_g_kv, mla_w_uk, mla_w_uv, mla_w_o, ret_w_q, ret_w_k, ret_w_v, ret_w_g, ret_gn_g, ret_gn_b, ret_w_o, ffn_w1, ffn_w3, ffn_w2):
    batch, seq, d = x_prompt.shape
    db, n_new, _ = x_sample.shape
    depth = norm_mix.shape[0]
    past = page_table.shape[1] * PAGE_SIZE
    ns = db * n_new
    tm_p = min(512, seq)
    tm_s = min(256, ns)

    xp = x_prompt.reshape(batch * seq, d)
    xs = jnp.transpose(x_sample, (1, 0, 2)).reshape(ns, d)
    pos_p = jnp.arange(seq)
    pos_s = jnp.repeat(past + jnp.arange(n_new), db)

    def to_t_major(a):
        return jnp.swapaxes(a, 0, 1).reshape(ns, *a.shape[2:])

    def to_b_major(a):
        return jnp.swapaxes(a.reshape(n_new, db, *a.shape[1:]), 0, 1)

    conv_p, conv_s, lat_p, kr_p, lat_s, kr_s, ret_p, ret_s = [], [], [], [], [], [], [], []
    for i in range(depth):
        j = i // N_MIXERS
        kind = i % N_MIXERS
        if kind == 0:
            cw = (norm_mix[i], conv_w_pw1[j].astype(BF16), conv_b_pw1[j], conv_w_dw[j], conv_b_dw[j],
                  conv_ln_g[j], conv_ln_b[j], conv_w_pw2[j].astype(BF16), conv_b_pw2[j])
            xp, st_p = _conv_prompt(xp, seq, *cw, tt=tm_p)
            xs3, st_s = _conv_sample(xs.reshape(n_new, db, d), jnp.swapaxes(state_conv[j], 0, 1), *cw,
                                     bb=min(32, db))
            xs = xs3.reshape(ns, d)
            conv_p.append(st_p)
            conv_s.append(jnp.swapaxes(st_s, 0, 1))
        elif kind == 1:
            w = _mla_weights(mla_w_dq[j], mla_g_q[j], mla_w_uq[j], mla_w_dkv[j], mla_g_kv[j], mla_w_uk[j],
                             mla_w_uv[j], mla_w_o[j])
            q, k, kt, ckv, kr = _mla_proj(xp, norm_mix[i], w, *_mla_tables(pos_p), tm=tm_p, transposed=True)
            a = _mla_attn_prompt(q, k, kt, seq, tq=min(512, seq))
            xp = _mla_out(xp, a, w["uv"], w["o"], tm=tm_p)
            lat_p.append(ckv.reshape(batch, seq, MLA_KV_LORA))
            kr_p.append(kr.reshape(batch, seq, MLA_ROPE))

            q, k, ckv, kr = _mla_proj(xs, norm_mix[i], w, *_mla_tables(pos_s), tm=tm_s, transposed=False)
            q_b = jnp.transpose(q.reshape(MLA_HEADS, n_new, db, MLA_QK), (2, 0, 1, 3)).reshape(
                db, MLA_HEADS * n_new, MLA_QK)
            k_b = jnp.pad(to_b_major(k).astype(F32), ((0, 0), (0, 8 - n_new), (0, 0)))
            a_b = _mla_attn_sample(q_b, k_b, cache_mla_latent[j], jnp.swapaxes(cache_mla_krope[j], 1, 2),
                                   page_table, n_new)
            a = jnp.transpose(a_b.reshape(db, MLA_HEADS, n_new, MLA_KV_LORA), (1, 2, 0, 3)).reshape(
                MLA_HEADS, ns, MLA_KV_LORA)
            xs = _mla_out(xs, a, w["uv"], w["o"], tm=tm_s)
            lat_s.append(to_b_major(ckv))
            kr_s.append(to_b_major(kr))
        else:
            w = {"q": ret_w_q[j].astype(BF16), "k": ret_w_k[j].astype(BF16), "v": ret_w_v[j].astype(BF16),
                 "g": ret_w_g[j].astype(BF16), "gng": ret_gn_g[j], "gnb": ret_gn_b[j],
                 "o": ret_w_o[j].astype(BF16)}
            xp, st_p = _ret_prompt(xp, seq, norm_mix[i], w, *_rope_tables(pos_p, RET_DK), tt=tm_p,
                                   lc=min(256, seq))
            ret_p.append(st_p)

            q, k, v, g = _ret_proj(xs, norm_mix[i], w, *_rope_tables(pos_s, RET_DK), tm=tm_s)
            pad = lambda a: jnp.pad(to_b_major(a), ((0, 0), (0, 8 - n_new), (0, 0)))
            o_b, st_s = _ret_sample(pad(q), pad(k), pad(v), state_ret[j], n_new)
            xs = _ret_out(xs, to_t_major(o_b[:, :n_new]), g, w["gng"], w["gnb"], w["o"], tm=tm_s)
            ret_s.append(st_s)
        last = i == depth - 1
        fw = (norm_ffn[i], ffn_w1[i].astype(BF16), ffn_w3[i].astype(BF16), ffn_w2[i].astype(BF16),
              norm_final if last else None)
        xp = _ffn(xp, *fw, tm=tm_p)
        xs = _ffn(xs, *fw, tm=tm_s)

    return (xp.reshape(batch, seq, d), to_b_major(xs),
            jnp.stack(conv_p), jnp.stack(conv_s),
            jnp.stack(lat_p), jnp.stack(kr_p), jnp.stack(lat_s), jnp.stack(kr_s),
            jnp.stack(ret_p), jnp.stack(ret_s))
```

```python
import functools
import math

import jax
import jax.numpy as jnp
from jax import lax
from jax.experimental import pallas as pl
from jax.experimental.pallas import tpu as pltpu

F32 = jnp.float32
BF16 = jnp.bfloat16

EPS = 1e-6
NEG_INF = -1e30
ROPE_BASE = 10000.0
PAGE_SIZE = 128
N_MIXERS = 3
CONV_WIDTH = 31
CONV_PREV = CONV_WIDTH - 1
MLA_HEADS = 8
MLA_KV_LORA = 256
MLA_NOPE = 128
MLA_ROPE = 64
MLA_V = 128
MLA_QK = MLA_KV_LORA + 128
MLA_SCALE = (MLA_NOPE + MLA_ROPE) ** -0.5
MLA_QSCALE = MLA_SCALE * math.log2(math.e)
RET_DK = 256
RET_DV = 512

LANE = 128
SUBLANE = 8
MIB = 1024 * 1024


def _cparams(sem, vmem_mib):
    return pltpu.CompilerParams(dimension_semantics=sem, vmem_limit_bytes=vmem_mib * MIB)


def _const_spec(shape):
    nd = len(shape)
    return pl.BlockSpec(shape, lambda *_: (0,) * nd, pipeline_mode=pl.Buffered(1))


def _layer_spec(shape, layer):
    nd = len(shape)
    return pl.BlockSpec((None,) + tuple(shape), lambda *_: (layer,) + (0,) * nd, pipeline_mode=pl.Buffered(1))


def _mm(a, b):
    return jnp.dot(a, b, preferred_element_type=F32)


def _mm_nt(a, b):
    return lax.dot_general(a, b, (((1,), (1,)), ((), ())), preferred_element_type=F32)


def _rms(x, g):
    return x * lax.rsqrt(jnp.mean(x * x, axis=-1, keepdims=True) + EPS) * g


def _silu(x):
    return x * jax.nn.sigmoid(x)


def _ffn_kernel(x_ref, g_ref, w1_ref, w3_ref, w2_ref, *rest, final):
    if final:
        gf_ref, o_ref = rest
    else:
        (o_ref,) = rest
    x = x_ref[...]
    h = _rms(x, g_ref[...]).astype(BF16)
    a = _mm(h, w1_ref[...])
    b = _mm(h, w3_ref[...])
    y = x + _mm((_silu(a) * b).astype(BF16), w2_ref[...])
    if final:
        y = _rms(y, gf_ref[...])
    o_ref[...] = y


def _ffn(x, g, w1, w3, w2, layer, g_final=None, *, tm):
    n, d = x.shape
    dff = w1.shape[2]
    assert n % tm == 0
    final = g_final is not None
    in_specs = [
        pl.BlockSpec((tm, d), lambda i: (i, 0)),
        _const_spec((1, d)),
        _layer_spec((d, dff), layer), _layer_spec((d, dff), layer), _layer_spec((dff, d), layer),
    ]
    args = [x, g.reshape(1, d), w1, w3, w2]
    if final:
        in_specs.append(_const_spec((1, d)))
        args.append(g_final.reshape(1, d))
    return pl.pallas_call(
        functools.partial(_ffn_kernel, final=final),
        grid=(n // tm,),
        in_specs=in_specs,
        out_specs=pl.BlockSpec((tm, d), lambda i: (i, 0)),
        out_shape=jax.ShapeDtypeStruct((n, d), F32),
        compiler_params=_cparams(("parallel",), 56),
        name="ffn_final" if final else "ffn",
    )(*args)


def _conv_glu(x, gm, w1, b1):
    d = x.shape[-1]
    a = _mm(_rms(x, gm).astype(BF16), w1) + b1
    return a[:, :d] * jax.nn.sigmoid(a[:, d:])


def _conv_tail(c, lng, lnb, w2, b2):
    mu = jnp.mean(c, axis=-1, keepdims=True)
    cc = c - mu
    cn = cc * lax.rsqrt(jnp.mean(cc * cc, axis=-1, keepdims=True) + EPS) * lng + lnb
    return _mm(_silu(cn).astype(BF16), w2) + b2


CONV_CARRY = 32
CONV_ROWS = 64
CONV_COLS = 256


def _conv_prompt_kernel(x_ref, gm_ref, w1_ref, b1_ref, wdw_ref, bdw_ref, lng_ref, lnb_ref, w2_ref, b2_ref,
                        o_ref, st_ref, ubuf_ref, c_ref):
    tt, d = x_ref.shape
    t = pl.program_id(1)
    tail = tt + CONV_CARRY - SUBLANE

    @pl.when(t == 0)
    def _():
        for s in range(SUBLANE):
            ubuf_ref[s, 0:CONV_CARRY, :] = jnp.zeros((CONV_CARRY, d), F32)
            ubuf_ref[s, tail:tail + SUBLANE, :] = jnp.zeros((SUBLANE, d), F32)

    x = x_ref[...]
    u = _conv_glu(x, gm_ref[...], w1_ref[...], b1_ref[...])
    for s in range(SUBLANE):
        ubuf_ref[s, CONV_CARRY - s:CONV_CARRY - s + tt, :] = u

    lead = CONV_CARRY - CONV_PREV
    for r0 in range(0, tt, CONV_ROWS):
        for c0 in range(0, d, CONV_COLS):
            cols = slice(c0, c0 + CONV_COLS)
            acc = jnp.broadcast_to(bdw_ref[:, cols], (CONV_ROWS, CONV_COLS))
            for k in range(CONV_WIDTH):
                s = (lead + k) % SUBLANE
                row = r0 + lead + k - s
                acc = acc + ubuf_ref[s, row:row + CONV_ROWS, cols] * wdw_ref[k:k + 1, cols]
            c_ref[r0:r0 + CONV_ROWS, cols] = acc

    y = _conv_tail(c_ref[...], lng_ref[...], lnb_ref[...], w2_ref[...], b2_ref[...])
    o_ref[...] = x + y
    st_ref[...] = ubuf_ref[0, tt + lead:tt + CONV_CARRY, :]
    for s in range(SUBLANE):
        ubuf_ref[s, 0:CONV_CARRY, :] = ubuf_ref[s, tt:tt + CONV_CARRY, :]


def _conv_prompt(x, seq, gm, w1, b1, wdw, bdw, lng, lnb, w2, b2, *, tt):
    n, d = x.shape
    batch = n // seq
    nt = seq // tt
    assert seq % tt == 0 and tt % CONV_ROWS == 0 and d % CONV_COLS == 0
    row = lambda a: a.reshape(1, -1)
    return pl.pallas_call(
        _conv_prompt_kernel,
        grid=(batch, nt),
        in_specs=[
            pl.BlockSpec((tt, d), lambda b, t: (b * nt + t, 0)),
            _const_spec((1, d)), _const_spec((d, 2 * d)), _const_spec((1, 2 * d)),
            _const_spec((CONV_WIDTH, d)), _const_spec((1, d)), _const_spec((1, d)), _const_spec((1, d)),
            _const_spec((d, d)), _const_spec((1, d)),
        ],
        out_specs=[
            pl.BlockSpec((tt, d), lambda b, t: (b * nt + t, 0)),
            pl.BlockSpec((None, CONV_PREV, d), lambda b, t: (b, 0, 0)),
        ],
        out_shape=[jax.ShapeDtypeStruct((n, d), F32), jax.ShapeDtypeStruct((batch, CONV_PREV, d), F32)],
        scratch_shapes=[pltpu.VMEM((SUBLANE, tt + CONV_CARRY, d), F32), pltpu.VMEM((tt, d), F32)],
        compiler_params=_cparams(("parallel", "arbitrary"), 56),
        name="conv_prompt",
    )(x, row(gm), w1, row(b1), wdw, row(bdw), row(lng), row(lnb), w2, row(b2))


def _conv_sample_kernel(x_ref, st_ref, gm_ref, w1_ref, b1_ref, wdw_ref, bdw_ref, lng_ref, lnb_ref, w2_ref, b2_ref,
                        o_ref, ns_ref):
    nt, bb, d = x_ref.shape
    x = x_ref[...].reshape(nt * bb, d)
    u = _conv_glu(x, gm_ref[...], w1_ref[...], b1_ref[...])
    us = [u[i * bb:(i + 1) * bb] for i in range(nt)]

    def buf(j):
        return st_ref[j] if j < CONV_PREV else us[j - CONV_PREV]

    cs = []
    for t in range(nt):
        acc = jnp.broadcast_to(bdw_ref[...], (bb, d))
        for k in range(CONV_WIDTH):
            acc = acc + buf(t + k) * wdw_ref[k:k + 1, :]
        cs.append(acc)
    c = jnp.concatenate(cs, axis=0)
    y = _conv_tail(c, lng_ref[...], lnb_ref[...], w2_ref[...], b2_ref[...])
    o_ref[...] = (x + y).reshape(nt, bb, d)
    keep = CONV_PREV - nt
    ns_ref[0:keep] = st_ref[nt:CONV_PREV]
    for i in range(nt):
        ns_ref[keep + i] = us[i]


def _conv_sample(x3, state, gm, w1, b1, wdw, bdw, lng, lnb, w2, b2, *, bb):
    nt, db, d = x3.shape
    assert db % bb == 0 and nt <= CONV_PREV
    row = lambda a: a.reshape(1, -1)
    return pl.pallas_call(
        _conv_sample_kernel,
        grid=(db // bb,),
        in_specs=[
            pl.BlockSpec((nt, bb, d), lambda i: (0, i, 0)),
            pl.BlockSpec((CONV_PREV, bb, d), lambda i: (0, i, 0)),
            _const_spec((1, d)), _const_spec((d, 2 * d)), _const_spec((1, 2 * d)),
            _const_spec((CONV_WIDTH, d)), _const_spec((1, d)), _const_spec((1, d)), _const_spec((1, d)),
            _const_spec((d, d)), _const_spec((1, d)),
        ],
        out_specs=[
            pl.BlockSpec((nt, bb, d), lambda i: (0, i, 0)),
            pl.BlockSpec((CONV_PREV, bb, d), lambda i: (0, i, 0)),
        ],
        out_shape=[jax.ShapeDtypeStruct((nt, db, d), F32), jax.ShapeDtypeStruct((CONV_PREV, db, d), F32)],
        compiler_params=_cparams(("parallel",), 48),
        name="conv_sample",
    )(x3, state, row(gm), w1, row(b1), wdw, row(bdw), row(lng), row(lnb), w2, row(b2))


def _mla_proj_kernel(x_ref, gm_ref, wdq_ref, gq_ref, wuq_ref, wdkv_ref, gkv_ref, wuk_ref, cos_ref, sin_ref,
                     *rest, transposed):
    if transposed:
        cos_t_ref, sin_t_ref, q_ref, k_ref, kt_ref, ckv_ref, kr_ref = rest
    else:
        q_ref, k_ref, ckv_ref, kr_ref = rest
    h = _rms(x_ref[...], gm_ref[...]).astype(BF16)
    cos = cos_ref[...]
    sin = sin_ref[...]
    cq = _rms(_mm(h, wdq_ref[...]), gq_ref[...])
    nh = q_ref.shape[0]
    if transposed:
        qt = _mm(wuq_ref[...], cq.T.astype(BF16))
        cos_t = cos_t_ref[...]
        sin_t = sin_t_ref[...]
        for i in range(nh):
            base = i * 3 * LANE
            q_lat = _mm(wuk_ref[i], qt[base:base + LANE].astype(BF16))
            q_rot = qt[base + LANE:base + 2 * LANE] * cos_t + qt[base + 2 * LANE:base + 3 * LANE] * sin_t
            q_ref[i, 0:MLA_KV_LORA, :] = (q_lat * MLA_QSCALE).astype(BF16)
            q_ref[i, MLA_KV_LORA:MLA_QK, :] = (q_rot * MLA_QSCALE).astype(BF16)
    else:
        q = _mm(cq.astype(BF16), wuq_ref[...])
        for i in range(nh):
            base = i * 3 * LANE
            q_lat = _mm(q[:, base:base + LANE].astype(BF16), wuk_ref[i])
            q_rot = q[:, base + LANE:base + 2 * LANE] * cos + q[:, base + 2 * LANE:base + 3 * LANE] * sin
            q_ref[i, :, 0:MLA_KV_LORA] = (q_lat * MLA_QSCALE).astype(BF16)
            q_ref[i, :, MLA_KV_LORA:MLA_QK] = (q_rot * MLA_QSCALE).astype(BF16)
    a = _mm(h, wdkv_ref[...])
    ckv = _rms(a[:, 0:MLA_KV_LORA], gkv_ref[...])
    k_rot = a[:, MLA_KV_LORA:MLA_QK] * cos + a[:, MLA_QK:MLA_QK + LANE] * sin
    ckv_ref[...] = ckv
    kr_ref[...] = k_rot[:, 0:MLA_ROPE]
    k_ref[:, 0:MLA_KV_LORA] = ckv.astype(BF16)
    k_ref[:, MLA_KV_LORA:MLA_QK] = k_rot.astype(BF16)
    if transposed:
        kt_ref[...] = jnp.concatenate([ckv, k_rot], axis=1).T.astype(BF16)


def _mla_proj(x, gm, w, cos, sin, *, tm, transposed):
    n, d = x.shape
    period = cos.shape[0] // tm
    assert n % tm == 0 and cos.shape[0] % tm == 0
    ql = w["dq"].shape[1]
    wuq, wuk = (w["uq_t"], w["uk_t"]) if transposed else (w["uq"], w["uk"])
    tok = lambda width: pl.BlockSpec((tm, width), lambda i: (i, 0))
    in_specs = [
        tok(d), _const_spec((1, d)), _const_spec((d, ql)), _const_spec((1, ql)),
        _const_spec(wuq.shape), _const_spec(w["dkv"].shape), _const_spec((1, MLA_KV_LORA)), _const_spec(wuk.shape),
        pl.BlockSpec((tm, LANE), lambda i: (i % period, 0)),
        pl.BlockSpec((tm, LANE), lambda i: (i % period, 0)),
    ]
    args = [x, gm.reshape(1, d), w["dq"], w["gq"].reshape(1, ql), wuq, w["dkv"], w["gkv"].reshape(1, MLA_KV_LORA),
            wuk, cos, sin]
    out_specs = [tok(MLA_QK), tok(MLA_KV_LORA), tok(MLA_ROPE)]
    out_shape = [jax.ShapeDtypeStruct((n, MLA_QK), BF16), jax.ShapeDtypeStruct((n, MLA_KV_LORA), F32),
                 jax.ShapeDtypeStruct((n, MLA_ROPE), F32)]
    if transposed:
        in_specs += [pl.BlockSpec((LANE, tm), lambda i: (0, i % period))] * 2
        args += [cos.T, sin.T]
        out_specs = ([pl.BlockSpec((MLA_HEADS, MLA_QK, tm), lambda i: (0, 0, i)), out_specs[0],
                      pl.BlockSpec((MLA_QK, tm), lambda i: (0, i))] + out_specs[1:])
        out_shape = ([jax.ShapeDtypeStruct((MLA_HEADS, MLA_QK, n), BF16), out_shape[0],
                      jax.ShapeDtypeStruct((MLA_QK, n), BF16)] + out_shape[1:])
    else:
        out_specs = [pl.BlockSpec((MLA_HEADS, tm, MLA_QK), lambda i: (0, i, 0))] + out_specs
        out_shape = [jax.ShapeDtypeStruct((MLA_HEADS, n, MLA_QK), BF16)] + out_shape
    return pl.pallas_call(
        functools.partial(_mla_proj_kernel, transposed=transposed),
        grid=(n // tm,),
        in_specs=in_specs,
        out_specs=out_specs,
        out_shape=out_shape,
        compiler_params=_cparams(("parallel",), 48),
        name="mla_proj_t" if transposed else "mla_proj",
    )(*args)


def _mla_attn_prompt_kernel(qi_ref, kj_ref, qt_ref, k_ref, vt_ref, o_ref, m_ref, l_ref, acc_ref):
    nh, dq, tq = qt_ref.shape
    tk = k_ref.shape[0]
    p = pl.program_id(1)
    i = qi_ref[p]
    j = kj_ref[p]

    @pl.when(j == 0)
    def _():
        m_ref[...] = jnp.full(m_ref.shape, NEG_INF, F32)
        l_ref[...] = jnp.zeros(l_ref.shape, F32)
        acc_ref[...] = jnp.zeros(acc_ref.shape, F32)

    def scores(h):
        return _mm(k_ref[...], qt_ref[h])

    def update(h, s, diagonal):
        if diagonal:
            causal = lax.broadcasted_iota(jnp.int32, (tk, tq), 0) <= lax.broadcasted_iota(jnp.int32, (tk, tq), 1)
            s = jnp.where(causal, s, NEG_INF)
        m_prev = m_ref[h]
        m_new = jnp.maximum(m_prev, jnp.max(s, axis=0, keepdims=True))
        alpha = jnp.exp2(m_prev - m_new)
        e = jnp.exp2(s - m_new)
        l_new = alpha * l_ref[h] + jnp.sum(e, axis=0, keepdims=True)
        acc = alpha * acc_ref[h] + _mm(vt_ref[...], e.astype(BF16))
        if diagonal:
            o_ref[h] = (acc / l_new).T.astype(o_ref.dtype)
        else:
            m_ref[h] = m_new
            l_ref[h] = l_new
            acc_ref[h] = acc

    def all_heads(diagonal):
        s_next = scores(0)
        for h in range(nh):
            s = s_next
            if h + 1 < nh:
                s_next = scores(h + 1)
            update(h, s, diagonal)

    @pl.when(j < i)
    def _():
        all_heads(False)

    @pl.when(j == i)
    def _():
        all_heads(True)


def _mla_attn_prompt(qt, k, kt, seq, *, tq):
    nh, dq, n = qt.shape
    batch = n // seq
    nb = seq // tq
    assert seq % tq == 0
    pairs = [(i, j) for i in range(nb) for j in range(i + 1)]
    qi = jnp.asarray([p[0] for p in pairs], jnp.int32)
    kj = jnp.asarray([p[1] for p in pairs], jnp.int32)
    grid_spec = pltpu.PrefetchScalarGridSpec(
        num_scalar_prefetch=2,
        grid=(batch, len(pairs)),
        in_specs=[
            pl.BlockSpec((nh, dq, tq), lambda b, p, qi, kj: (0, 0, b * nb + qi[p])),
            pl.BlockSpec((tq, dq), lambda b, p, qi, kj: (b * nb + kj[p], 0)),
            pl.BlockSpec((MLA_KV_LORA, tq), lambda b, p, qi, kj: (0, b * nb + kj[p])),
        ],
        out_specs=pl.BlockSpec((nh, tq, MLA_KV_LORA), lambda b, p, qi, kj: (0, b * nb + qi[p], 0)),
        scratch_shapes=[pltpu.VMEM((nh, 1, tq), F32), pltpu.VMEM((nh, 1, tq), F32),
                        pltpu.VMEM((nh, MLA_KV_LORA, tq), F32)],
    )
    return pl.pallas_call(
        _mla_attn_prompt_kernel,
        grid_spec=grid_spec,
        out_shape=jax.ShapeDtypeStruct((nh, n, MLA_KV_LORA), BF16),
        compiler_params=_cparams(("parallel", "arbitrary"), 48),
        name="mla_attn_prompt",
    )(qi, kj, qt, k, kt)


MLA_SAMPLE_CHUNK = 1024
MLA_NEW_PAD = 128
MLA_PAGE_UNROLL = 8


def _mla_attn_sample_kernel(pt_ref, q_ref, kn_ref, lat_hbm, kr_hbm, o_ref, lat_buf, kr_buf, lat_bf, s_ref, sem, *,
                            n_new):
    b = pl.program_id(0)
    nb = pl.num_programs(0)
    past = lat_buf.shape[1]
    n_pages = past // PAGE_SIZE
    rows = q_ref.shape[0]
    slot = b % 2

    def page_copies(bi, sl, pg):
        page = pt_ref[bi * n_pages + pg]
        dst = pl.ds(pg * PAGE_SIZE, PAGE_SIZE)
        return (pltpu.make_async_copy(lat_hbm.at[page], lat_buf.at[sl, dst], sem.at[0, sl]),
                pltpu.make_async_copy(kr_hbm.at[page], kr_buf.at[sl, :, dst], sem.at[1, sl]))

    def for_each_page(bi, sl, act):
        def body(g, _):
            for u in range(MLA_PAGE_UNROLL):
                for c in page_copies(bi, sl, g * MLA_PAGE_UNROLL + u):
                    act(c)
            return 0
        lax.fori_loop(0, n_pages // MLA_PAGE_UNROLL, body, 0)

    @pl.when(b == 0)
    def _():
        for_each_page(0, 0, lambda c: c.start())

    @pl.when(b + 1 < nb)
    def _():
        for_each_page(b + 1, 1 - slot, lambda c: c.start())

    for_each_page(b, slot, lambda c: c.wait())

    q = q_ref[...]
    q_lat = q[:, 0:MLA_KV_LORA]
    q_rope = q[:, MLA_KV_LORA:MLA_KV_LORA + MLA_ROPE]
    ch = MLA_SAMPLE_CHUNK
    n_chunks = past // ch

    m = jnp.full((rows, 1), NEG_INF, F32)
    for c in range(n_chunks):
        keys = slice(c * ch, (c + 1) * ch)
        lat = lat_buf[slot, keys, :].astype(BF16)
        lat_bf[keys, :] = lat
        s = _mm_nt(q_lat, lat) + _mm(q_rope, kr_buf[slot, :, keys].astype(BF16))
        s_ref[:, keys] = s
        m = jnp.maximum(m, jnp.max(s, axis=-1, keepdims=True))

    kn = kn_ref[...]
    k_new = jnp.concatenate([kn, jnp.zeros((MLA_NEW_PAD - kn.shape[0], kn.shape[1]), F32)], axis=0).astype(BF16)
    s_new = _mm_nt(q, k_new)
    r_tok = lax.broadcasted_iota(jnp.int32, (rows, MLA_NEW_PAD), 0) % n_new
    s_new = jnp.where(lax.broadcasted_iota(jnp.int32, (rows, MLA_NEW_PAD), 1) <= r_tok, s_new, NEG_INF)
    m = jnp.maximum(m, jnp.max(s_new, axis=-1, keepdims=True))
    e_new = jnp.exp2(s_new - m)
    l0 = jnp.sum(e_new, axis=-1, keepdims=True)
    acc0 = _mm(e_new.astype(BF16), k_new[:, 0:MLA_KV_LORA])

    l, acc = l0, acc0
    for c in range(n_chunks):
        keys = slice(c * ch, (c + 1) * ch)
        e = jnp.exp2(s_ref[:, keys] - m)
        l = l + jnp.sum(e, axis=-1, keepdims=True)
        acc = acc + _mm(e.astype(BF16), lat_bf[keys, :])
    o_ref[...] = (acc / l).astype(o_ref.dtype)


def _mla_attn_sample(q, k_new, cache_lat, cache_kr, page_table, n_new):
    db, rows, dq = q.shape
    tp = k_new.shape[1]
    n_pages = page_table.shape[1]
    past = n_pages * PAGE_SIZE
    assert past % MLA_SAMPLE_CHUNK == 0 and n_pages % MLA_PAGE_UNROLL == 0
    grid_spec = pltpu.PrefetchScalarGridSpec(
        num_scalar_prefetch=1,
        grid=(db,),
        in_specs=[
            pl.BlockSpec((None, rows, dq), lambda b, pt: (b, 0, 0)),
            pl.BlockSpec((None, tp, dq), lambda b, pt: (b, 0, 0)),
            pl.BlockSpec(memory_space=pl.ANY),
            pl.BlockSpec(memory_space=pl.ANY),
        ],
        out_specs=pl.BlockSpec((None, rows, MLA_KV_LORA), lambda b, pt: (b, 0, 0)),
        scratch_shapes=[
            pltpu.VMEM((2, past, MLA_KV_LORA), F32),
            pltpu.VMEM((2, MLA_ROPE, past), F32),
            pltpu.VMEM((past, MLA_KV_LORA), BF16),
            pltpu.VMEM((rows, past), F32),
            pltpu.SemaphoreType.DMA((2, 2)),
        ],
    )
    return pl.pallas_call(
        functools.partial(_mla_attn_sample_kernel, n_new=n_new),
        grid_spec=grid_spec,
        out_shape=jax.ShapeDtypeStruct((db, rows, MLA_KV_LORA), BF16),
        compiler_params=_cparams(("arbitrary",), 48),
        name="mla_attn_sample",
    )(page_table.reshape(-1), q, k_new, cache_lat, cache_kr)


def _mla_out_kernel(x_ref, a_ref, wuv_ref, wo_ref, o_ref):
    nh = a_ref.shape[0]
    o = jnp.concatenate([_mm(a_ref[i], wuv_ref[i]).astype(BF16) for i in range(nh)], axis=-1)
    o_ref[...] = x_ref[...] + _mm(o, wo_ref[...])


def _mla_out(x, a, wuv, wo, *, tm):
    n, d = x.shape
    assert n % tm == 0
    return pl.pallas_call(
        _mla_out_kernel,
        grid=(n // tm,),
        in_specs=[
            pl.BlockSpec((tm, d), lambda i: (i, 0)),
            pl.BlockSpec((MLA_HEADS, tm, MLA_KV_LORA), lambda i: (0, i, 0)),
            _const_spec(wuv.shape), _const_spec(wo.shape),
        ],
        out_specs=pl.BlockSpec((tm, d), lambda i: (i, 0)),
        out_shape=jax.ShapeDtypeStruct((n, d), F32),
        compiler_params=_cparams(("parallel",), 48),
        name="mla_out",
    )(x, a, wuv, wo)


def _ret_project(x, gm, wq, wk, wv, wg, cos, sin, n_heads):
    h = _rms(x, gm).astype(BF16)
    half = RET_DK // 2

    def rot(a):
        outs = []
        for i in range(n_heads):
            a1 = a[:, i * RET_DK:i * RET_DK + half]
            a2 = a[:, i * RET_DK + half:(i + 1) * RET_DK]
            outs += [a1 * cos - a2 * sin, a2 * cos + a1 * sin]
        return jnp.concatenate(outs, axis=-1)

    return rot(_mm(h, wq)), rot(_mm(h, wk)) * (RET_DK ** -0.5), _mm(h, wv), _mm(h, wg)


def _ret_gate(o, g, gng, gnb):
    mu = jnp.mean(o, axis=-1, keepdims=True)
    oc = o - mu
    on = oc * lax.rsqrt(jnp.mean(oc * oc, axis=-1, keepdims=True) + EPS)
    return (_silu(g) * (on * gng + gnb)).astype(BF16)


def _ret_prompt_kernel(x_ref, gm_ref, wq_ref, wk_ref, wv_ref, wg_ref, cos_ref, sin_ref,
                       decay_ref, cross_ref, kdec_ref, gl_ref, gng_ref, gnb_ref, wo_ref,
                       o_ref, st_ref, state_ref, y_ref):
    tt, d = x_ref.shape
    nh, lc, _ = decay_ref.shape
    c = pl.program_id(1)

    @pl.when(c == 0)
    def _():
        state_ref[...] = jnp.zeros(state_ref.shape, F32)

    x = x_ref[...]
    q, k, v, g = _ret_project(x, gm_ref[...], wq_ref[...], wk_ref[...], wv_ref[...], wg_ref[...],
                              cos_ref[...], sin_ref[...], nh)
    y_ref[...] = x
    for r0 in range(0, tt, lc):
        rows = slice(r0, r0 + lc)
        for i in range(nh):
            qh = q[rows, i * RET_DK:(i + 1) * RET_DK].astype(BF16)
            kh = k[rows, i * RET_DK:(i + 1) * RET_DK]
            vh = v[rows, i * RET_DV:(i + 1) * RET_DV].astype(BF16)
            state = state_ref[i]
            inner = _mm_nt(qh, kh.astype(BF16)) * decay_ref[i]
            o = _mm(inner.astype(BF16), vh) + _mm(qh, state.astype(BF16)) * cross_ref[i]
            k_dec = (kh * kdec_ref[i]).T.astype(BF16)
            state_ref[i] = state * gl_ref[i] + _mm(k_dec, vh)
            hv = slice(i * RET_DV, (i + 1) * RET_DV)
            gated = _ret_gate(o, g[rows, hv], gng_ref[:, hv], gnb_ref[:, hv])
            y_ref[rows, :] += _mm(gated, wo_ref[hv, :])
    o_ref[...] = y_ref[...]

    @pl.when(c == pl.num_programs(1) - 1)
    def _():
        st_ref[...] = state_ref[...]


def _ret_tables(n_heads, lc):
    log_gamma = jnp.log(1.0 - 2.0 ** (-5.0 - jnp.arange(n_heads, dtype=F32)))
    n = jnp.arange(lc, dtype=F32)
    diff = n[:, None] - n[None, :]
    decay = jnp.exp(jnp.where(diff[None] >= 0, diff[None] * log_gamma[:, None, None], -jnp.inf))
    cross = jnp.exp((n[None, :] + 1.0) * log_gamma[:, None])[:, :, None]
    kdec = jnp.exp((lc - 1.0 - n)[None, :] * log_gamma[:, None])[:, :, None]
    gl = jnp.exp(lc * log_gamma)
    return decay, cross, kdec, gl


def _ret_prompt(x, seq, gm, w, cos, sin, *, tt, lc):
    n, d = x.shape
    batch = n // seq
    nt = seq // tt
    nh = w["q"].shape[1] // RET_DK
    assert seq % tt == 0 and tt % lc == 0
    decay, cross, kdec, gl = _ret_tables(nh, lc)
    gl = jnp.broadcast_to(gl[:, None, None], (nh, 1, RET_DV))
    dv = nh * RET_DV
    return pl.pallas_call(
        _ret_prompt_kernel,
        grid=(batch, nt),
        in_specs=[
            pl.BlockSpec((tt, d), lambda b, c: (b * nt + c, 0)),
            _const_spec((1, d)),
            _const_spec(w["q"].shape), _const_spec(w["k"].shape), _const_spec(w["v"].shape),
            _const_spec(w["g"].shape),
            pl.BlockSpec((tt, RET_DK // 2), lambda b, c: (c, 0)),
            pl.BlockSpec((tt, RET_DK // 2), lambda b, c: (c, 0)),
            _const_spec(decay.shape), _const_spec(cross.shape), _const_spec(kdec.shape), _const_spec(gl.shape),
            _const_spec((1, dv)), _const_spec((1, dv)), _const_spec(w["o"].shape),
        ],
        out_specs=[
            pl.BlockSpec((tt, d), lambda b, c: (b * nt + c, 0)),
            pl.BlockSpec((None, nh, RET_DK, RET_DV), lambda b, c: (b, 0, 0, 0)),
        ],
        out_shape=[jax.ShapeDtypeStruct((n, d), F32), jax.ShapeDtypeStruct((batch, nh, RET_DK, RET_DV), F32)],
        scratch_shapes=[pltpu.VMEM((nh, RET_DK, RET_DV), F32), pltpu.VMEM((tt, d), F32)],
        compiler_params=_cparams(("parallel", "arbitrary"), 56),
        name="ret_prompt",
    )(x, gm.reshape(1, d), w["q"], w["k"], w["v"], w["g"], cos, sin, decay, cross, kdec, gl,
      w["gng"].reshape(1, dv), w["gnb"].reshape(1, dv), w["o"])


def _ret_proj_kernel(x_ref, gm_ref, wq_ref, wk_ref, wv_ref, wg_ref, cos_ref, sin_ref, q_ref, k_ref, v_ref, g_ref):
    nh = wq_ref.shape[1] // RET_DK
    q, k, v, g = _ret_project(x_ref[...], gm_ref[...], wq_ref[...], wk_ref[...], wv_ref[...], wg_ref[...],
                              cos_ref[...], sin_ref[...], nh)
    q_ref[...] = q
    k_ref[...] = k
    v_ref[...] = v
    g_ref[...] = g


def _ret_proj(x, gm, w, cos, sin, *, tm):
    n, d = x.shape
    dk = w["q"].shape[1]
    dv = w["v"].shape[1]
    assert n % tm == 0
    tok = lambda width: pl.BlockSpec((tm, width), lambda i: (i, 0))
    return pl.pallas_call(
        _ret_proj_kernel,
        grid=(n // tm,),
        in_specs=[tok(d), _const_spec((1, d)), _const_spec(w["q"].shape), _const_spec(w["k"].shape),
                  _const_spec(w["v"].shape), _const_spec(w["g"].shape), tok(RET_DK // 2), tok(RET_DK // 2)],
        out_specs=[tok(dk), tok(dk), tok(dv), tok(dv)],
        out_shape=[jax.ShapeDtypeStruct((n, dk), F32), jax.ShapeDtypeStruct((n, dk), F32),
                   jax.ShapeDtypeStruct((n, dv), F32), jax.ShapeDtypeStruct((n, dv), F32)],
        compiler_params=_cparams(("parallel",), 48),
        name="ret_proj",
    )(x, gm.reshape(1, d), w["q"], w["k"], w["v"], w["g"], cos, sin)


RET_NEW_PAD = 128
RET_SAMPLE_BLOCK = 2


def _ret_sample_kernel(q_ref, k_ref, v_ref, st_ref, decay_ref, cross_ref, kdec_ref, gl_ref, o_ref, ns_ref):
    bb, tp, _ = q_ref.shape
    nh = st_ref.shape[1]
    zpad = RET_NEW_PAD - tp
    for b in range(bb):
        for i in range(nh):
            qh = q_ref[b, :, i * RET_DK:(i + 1) * RET_DK].astype(BF16)
            kh = jnp.concatenate([k_ref[b, :, i * RET_DK:(i + 1) * RET_DK], jnp.zeros((zpad, RET_DK), F32)], axis=0)
            vh = jnp.concatenate([v_ref[b, :, i * RET_DV:(i + 1) * RET_DV], jnp.zeros((zpad, RET_DV), F32)],
                                 axis=0).astype(BF16)
            kt = kh.T
            state = st_ref[b, i]
            inner = _mm(qh, kt.astype(BF16)) * decay_ref[i]
            o = _mm(inner.astype(BF16), vh) + _mm(qh, state.astype(BF16)) * cross_ref[i]
            o_ref[b, :, i * RET_DV:(i + 1) * RET_DV] = o
            ns_ref[b, i] = state * gl_ref[i] + _mm((kt * kdec_ref[i]).astype(BF16), vh)


def _ret_sample(q, k, v, state, n_new):
    db, tp, dk = q.shape
    dv = v.shape[2]
    nh = state.shape[1]
    log_gamma = jnp.log(1.0 - 2.0 ** (-5.0 - jnp.arange(nh, dtype=F32)))
    n = jnp.arange(tp, dtype=F32)
    m = jnp.arange(RET_NEW_PAD, dtype=F32)
    diff = n[:, None] - m[None, :]
    live = (diff >= 0) & (m[None, :] < n_new)
    decay = jnp.exp(jnp.where(live[None], diff[None] * log_gamma[:, None, None], -jnp.inf))
    cross = jnp.exp((n[None, :] + 1.0) * log_gamma[:, None])[:, :, None]
    kdec = jnp.where(m[None, :] < n_new, jnp.exp((n_new - 1.0 - m)[None, :] * log_gamma[:, None]), 0.0)[:, None, :]
    gl = jnp.broadcast_to(jnp.exp(n_new * log_gamma)[:, None, None], (nh, 1, RET_DV))
    bb = RET_SAMPLE_BLOCK if db % RET_SAMPLE_BLOCK == 0 else 1
    return pl.pallas_call(
        _ret_sample_kernel,
        grid=(db // bb,),
        in_specs=[
            pl.BlockSpec((bb, tp, dk), lambda b: (b, 0, 0)),
            pl.BlockSpec((bb, tp, dk), lambda b: (b, 0, 0)),
            pl.BlockSpec((bb, tp, dv), lambda b: (b, 0, 0)),
            pl.BlockSpec((bb, nh, RET_DK, RET_DV), lambda b: (b, 0, 0, 0)),
            _const_spec(decay.shape), _const_spec(cross.shape), _const_spec(kdec.shape), _const_spec(gl.shape),
        ],
        out_specs=[
            pl.BlockSpec((bb, tp, dv), lambda b: (b, 0, 0)),
            pl.BlockSpec((bb, nh, RET_DK, RET_DV), lambda b: (b, 0, 0, 0)),
        ],
        out_shape=[jax.ShapeDtypeStruct((db, tp, dv), F32), jax.ShapeDtypeStruct(state.shape, F32)],
        compiler_params=_cparams(("parallel",), 40),
        name="ret_sample",
    )(q, k, v, state, decay, cross, kdec, gl)


def _ret_out_kernel(x_ref, o_ref_in, g_ref, gng_ref, gnb_ref, wo_ref, o_ref):
    nh = o_ref_in.shape[1] // RET_DV
    gated = jnp.concatenate(
        [_ret_gate(o_ref_in[:, i * RET_DV:(i + 1) * RET_DV], g_ref[:, i * RET_DV:(i + 1) * RET_DV],
                   gng_ref[:, i * RET_DV:(i + 1) * RET_DV], gnb_ref[:, i * RET_DV:(i + 1) * RET_DV])
         for i in range(nh)], axis=-1)
    o_ref[...] = x_ref[...] + _mm(gated, wo_ref[...])


def _ret_out(x, o, g, gng, gnb, wo, *, tm):
    n, d = x.shape
    dv = o.shape[1]
    assert n % tm == 0
    tok = lambda width: pl.BlockSpec((tm, width), lambda i: (i, 0))
    return pl.pallas_call(
        _ret_out_kernel,
        grid=(n // tm,),
        in_specs=[tok(d), tok(dv), tok(dv), _const_spec((1, dv)), _const_spec((1, dv)), _const_spec(wo.shape)],
        out_specs=tok(d),
        out_shape=jax.ShapeDtypeStruct((n, d), F32),
        compiler_params=_cparams(("parallel",), 48),
        name="ret_out",
    )(x, o, g, gng.reshape(1, dv), gnb.reshape(1, dv), wo)


def _rope_tables(pos, dim):
    if dim == MLA_ROPE:
        inv = ROPE_BASE ** (-jnp.arange(0, dim, 2, dtype=F32) / dim)
    else:
        inv = 1.0 / (ROPE_BASE ** jnp.linspace(0.0, 1.0, dim // 2, dtype=F32))
    ang = pos.astype(F32)[:, None] * inv[None, :]
    return jnp.cos(ang), jnp.sin(ang)


def _mla_tables(pos):
    cos, sin = _rope_tables(pos, MLA_ROPE)
    z = jnp.zeros((pos.shape[0], LANE - MLA_ROPE), F32)
    return jnp.concatenate([cos, cos, z], axis=1), jnp.concatenate([sin, sin, z], axis=1)


def _rot_half_cols(w):
    half = w.shape[-1] // 2
    return jnp.concatenate([-w[..., half:], w[..., :half]], axis=-1)


def _pad_cols(w, width):
    return jnp.pad(w, [(0, 0)] * (w.ndim - 1) + [(0, width - w.shape[-1])])


def _mla_weights(w_dq, g_q, w_uq, w_dkv, g_kv, w_uk, w_uv, w_o):
    ql = w_dq.shape[1]
    uq = w_uq.reshape(ql, MLA_HEADS, MLA_NOPE + MLA_ROPE)
    rope = uq[..., MLA_NOPE:]
    uq = jnp.concatenate([uq[..., :MLA_NOPE], _pad_cols(rope, LANE), _pad_cols(_rot_half_cols(rope), LANE)], axis=-1)
    k_rope = w_dkv[:, MLA_KV_LORA:]
    dkv = jnp.concatenate([w_dkv[:, :MLA_KV_LORA], _pad_cols(k_rope, LANE), _pad_cols(_rot_half_cols(k_rope), LANE)],
                          axis=-1)
    uq = uq.reshape(ql, MLA_HEADS * 3 * LANE).astype(BF16)
    return {
        "dq": w_dq.astype(BF16), "gq": g_q, "uq": uq, "uq_t": uq.T,
        "dkv": dkv.astype(BF16), "gkv": g_kv,
        "uk": jnp.transpose(w_uk, (1, 2, 0)).astype(BF16),
        "uk_t": jnp.transpose(w_uk, (1, 0, 2)).astype(BF16),
        "uv": jnp.transpose(w_uv, (1, 0, 2)).astype(BF16),
        "o": w_o.astype(BF16),
    }


def kernel(x_prompt, x_sample, state_conv, cache_mla_latent, cache_mla_krope, state_ret, page_table, norm_mix, norm_ffn, norm_final, conv_w_pw1, conv_b_pw1, conv_w_dw, conv_b_dw, conv_ln_g, conv_ln_b, conv_w_pw2, conv_b_pw2, mla_w_dq, mla_g_q, mla_w_uq, mla_w_dkv, mla_g_kv, mla_w_uk, mla_w_uv, mla_w_o, ret_w_q, ret_w_k, ret_w_v, ret_w_g, ret_gn_g, ret_gn_b, ret_w_o, ffn_w1, ffn_w3, ffn_w2):
    batch, seq, d = x_prompt.shape
    db, n_new, _ = x_sample.shape
    depth = norm_mix.shape[0]
    past = page_table.shape[1] * PAGE_SIZE
    ns = db * n_new
    tm_p = min(512, seq)
    tm_s = min(256, ns)

    xp = x_prompt.reshape(batch * seq, d)
    xs = jnp.transpose(x_sample, (1, 0, 2)).reshape(ns, d)
    pos_p = jnp.arange(seq)
    pos_s = jnp.repeat(past + jnp.arange(n_new), db)

    def to_t_major(a):
        return jnp.swapaxes(a, 0, 1).reshape(ns, *a.shape[2:])

    def to_b_major(a):
        return jnp.swapaxes(a.reshape(n_new, db, *a.shape[1:]), 0, 1)

    ffn_w1_bf, ffn_w3_bf, ffn_w2_bf = ffn_w1.astype(BF16), ffn_w3.astype(BF16), ffn_w2.astype(BF16)
    conv_p, conv_s, lat_p, kr_p, lat_s, kr_s, ret_p, ret_s = [], [], [], [], [], [], [], []
    for i in range(depth):
        j = i // N_MIXERS
        kind = i % N_MIXERS
        if kind == 0:
            cw = (norm_mix[i], conv_w_pw1[j].astype(BF16), conv_b_pw1[j], conv_w_dw[j], conv_b_dw[j],
                  conv_ln_g[j], conv_ln_b[j], conv_w_pw2[j].astype(BF16), conv_b_pw2[j])
            xp, st_p = _conv_prompt(xp, seq, *cw, tt=tm_p)
            xs3, st_s = _conv_sample(xs.reshape(n_new, db, d), jnp.swapaxes(state_conv[j], 0, 1), *cw,
                                     bb=min(32, db))
            xs = xs3.reshape(ns, d)
            conv_p.append(st_p)
            conv_s.append(jnp.swapaxes(st_s, 0, 1))
        elif kind == 1:
            w = _mla_weights(mla_w_dq[j], mla_g_q[j], mla_w_uq[j], mla_w_dkv[j], mla_g_kv[j], mla_w_uk[j],
                             mla_w_uv[j], mla_w_o[j])
            q, k, kt, ckv, kr = _mla_proj(xp, norm_mix[i], w, *_mla_tables(pos_p), tm=tm_p, transposed=True)
            a = _mla_attn_prompt(q, k, kt, seq, tq=min(512, seq))
            xp = _mla_out(xp, a, w["uv"], w["o"], tm=tm_p)
            lat_p.append(ckv.reshape(batch, seq, MLA_KV_LORA))
            kr_p.append(kr.reshape(batch, seq, MLA_ROPE))

            q, k, ckv, kr = _mla_proj(xs, norm_mix[i], w, *_mla_tables(pos_s), tm=tm_s, transposed=False)
            q_b = jnp.transpose(q.reshape(MLA_HEADS, n_new, db, MLA_QK), (2, 0, 1, 3)).reshape(
                db, MLA_HEADS * n_new, MLA_QK)
            k_b = jnp.pad(to_b_major(k).astype(F32), ((0, 0), (0, 8 - n_new), (0, 0)))
            a_b = _mla_attn_sample(q_b, k_b, cache_mla_latent[j], jnp.swapaxes(cache_mla_krope[j], 1, 2),
                                   page_table, n_new)
            a = jnp.transpose(a_b.reshape(db, MLA_HEADS, n_new, MLA_KV_LORA), (1, 2, 0, 3)).reshape(
                MLA_HEADS, ns, MLA_KV_LORA)
            xs = _mla_out(xs, a, w["uv"], w["o"], tm=tm_s)
            lat_s.append(to_b_major(ckv))
            kr_s.append(to_b_major(kr))
        else:
            w = {"q": ret_w_q[j].astype(BF16), "k": ret_w_k[j].astype(BF16), "v": ret_w_v[j].astype(BF16),
                 "g": ret_w_g[j].astype(BF16), "gng": ret_gn_g[j], "gnb": ret_gn_b[j],
                 "o": ret_w_o[j].astype(BF16)}
            xp, st_p = _ret_prompt(xp, seq, norm_mix[i], w, *_rope_tables(pos_p, RET_DK), tt=tm_p,
                                   lc=min(256, seq))
            ret_p.append(st_p)

            q, k, v, g = _ret_proj(xs, norm_mix[i], w, *_rope_tables(pos_s, RET_DK), tm=tm_s)
            pad = lambda a: jnp.pad(to_b_major(a), ((0, 0), (0, 8 - n_new), (0, 0)))
            o_b, st_s = _ret_sample(pad(q), pad(k), pad(v), state_ret[j], n_new)
            xs = _ret_out(xs, to_t_major(o_b[:, :n_new]), g, w["gng"], w["gnb"], w["o"], tm=tm_s)
            ret_s.append(st_s)
        last = i == depth - 1
        fw = (norm_ffn[i], ffn_w1_bf, ffn_w3_bf, ffn_w2_bf, i, norm_final if last else None)
        xp = _ffn(xp, *fw, tm=tm_p)
        xs = _ffn(xs, *fw, tm=tm_s)

    return (xp.reshape(batch, seq, d), to_b_major(xs),
            jnp.stack(conv_p), jnp.stack(conv_s),
            jnp.stack(lat_p), jnp.stack(kr_p), jnp.stack(lat_s), jnp.stack(kr_s),
            jnp.stack(ret_p), jnp.stack(ret_s))
```

```python
import functools
import math

import jax
import jax.numpy as jnp
from jax import lax
from jax.experimental import pallas as pl
from jax.experimental.pallas import tpu as pltpu

F32 = jnp.float32
BF16 = jnp.bfloat16

EPS = 1e-6
NEG_INF = -1e30
ROPE_BASE = 10000.0
PAGE_SIZE = 128
N_MIXERS = 3
CONV_WIDTH = 31
CONV_PREV = CONV_WIDTH - 1
MLA_HEADS = 8
MLA_KV_LORA = 256
MLA_NOPE = 128
MLA_ROPE = 64
MLA_V = 128
MLA_QK = MLA_KV_LORA + 128
MLA_SCALE = (MLA_NOPE + MLA_ROPE) ** -0.5
MLA_QSCALE = MLA_SCALE * math.log2(math.e)
RET_DK = 256
RET_DV = 512

LANE = 128
SUBLANE = 8
MIB = 1024 * 1024


def _cparams(sem, vmem_mib):
    return pltpu.CompilerParams(dimension_semantics=sem, vmem_limit_bytes=vmem_mib * MIB)


def _const_spec(shape):
    nd = len(shape)
    return pl.BlockSpec(shape, lambda *_: (0,) * nd, pipeline_mode=pl.Buffered(1))


def _layer_spec(shape, layer):
    nd = len(shape)
    return pl.BlockSpec((None,) + tuple(shape), lambda *_: (layer,) + (0,) * nd, pipeline_mode=pl.Buffered(1))


def _mm(a, b):
    return jnp.dot(a, b, preferred_element_type=F32)


def _mm_nt(a, b):
    return lax.dot_general(a, b, (((1,), (1,)), ((), ())), preferred_element_type=F32)


def _rms(x, g):
    return x * lax.rsqrt(jnp.mean(x * x, axis=-1, keepdims=True) + EPS) * g


def _silu(x):
    return x * jax.nn.sigmoid(x)


def _ffn_kernel(x_ref, g_ref, w1_ref, w3_ref, w2_ref, *rest, final):
    if final:
        gf_ref, o_ref = rest
    else:
        (o_ref,) = rest
    x = x_ref[...]
    h = _rms(x, g_ref[...]).astype(BF16)
    a = _mm(h, w1_ref[...])
    b = _mm(h, w3_ref[...])
    y = x + _mm((_silu(a) * b).astype(BF16), w2_ref[...])
    if final:
        y = _rms(y, gf_ref[...])
    o_ref[...] = y


def _ffn(x, g, w1, w3, w2, layer, g_final=None, *, tm):
    n, d = x.shape
    dff = w1.shape[2]
    assert n % tm == 0
    final = g_final is not None
    in_specs = [
        pl.BlockSpec((tm, d), lambda i: (i, 0)),
        _const_spec((1, d)),
        _layer_spec((d, dff), layer), _layer_spec((d, dff), layer), _layer_spec((dff, d), layer),
    ]
    args = [x, g.reshape(1, d), w1, w3, w2]
    if final:
        in_specs.append(_const_spec((1, d)))
        args.append(g_final.reshape(1, d))
    return pl.pallas_call(
        functools.partial(_ffn_kernel, final=final),
        grid=(n // tm,),
        in_specs=in_specs,
        out_specs=pl.BlockSpec((tm, d), lambda i: (i, 0)),
        out_shape=jax.ShapeDtypeStruct((n, d), F32),
        compiler_params=_cparams(("parallel",), 56),
        name="ffn_final" if final else "ffn",
    )(*args)


def _conv_glu(x, gm, w1, b1):
    d = x.shape[-1]
    a = _mm(_rms(x, gm).astype(BF16), w1) + b1
    return a[:, :d] * jax.nn.sigmoid(a[:, d:])


def _conv_tail(c, lng, lnb, w2, b2):
    mu = jnp.mean(c, axis=-1, keepdims=True)
    cc = c - mu
    cn = cc * lax.rsqrt(jnp.mean(cc * cc, axis=-1, keepdims=True) + EPS) * lng + lnb
    return _mm(_silu(cn).astype(BF16), w2) + b2


CONV_CARRY = 32
CONV_ROWS = 64
CONV_COLS = 256


def _conv_prompt_kernel(x_ref, gm_ref, w1_ref, b1_ref, wdw_ref, bdw_ref, lng_ref, lnb_ref, w2_ref, b2_ref,
                        o_ref, st_ref, ubuf_ref, c_ref):
    tt, d = x_ref.shape
    t = pl.program_id(1)
    tail = tt + CONV_CARRY - SUBLANE

    @pl.when(t == 0)
    def _():
        for s in range(SUBLANE):
            ubuf_ref[s, 0:CONV_CARRY, :] = jnp.zeros((CONV_CARRY, d), F32)
            ubuf_ref[s, tail:tail + SUBLANE, :] = jnp.zeros((SUBLANE, d), F32)

    x = x_ref[...]
    u = _conv_glu(x, gm_ref[...], w1_ref[...], b1_ref[...])
    for s in range(SUBLANE):
        ubuf_ref[s, CONV_CARRY - s:CONV_CARRY - s + tt, :] = u

    lead = CONV_CARRY - CONV_PREV
    for r0 in range(0, tt, CONV_ROWS):
        for c0 in range(0, d, CONV_COLS):
            cols = slice(c0, c0 + CONV_COLS)
            acc = jnp.broadcast_to(bdw_ref[:, cols], (CONV_ROWS, CONV_COLS))
            for k in range(CONV_WIDTH):
                s = (lead + k) % SUBLANE
                row = r0 + lead + k - s
                acc = acc + ubuf_ref[s, row:row + CONV_ROWS, cols] * wdw_ref[k:k + 1, cols]
            c_ref[r0:r0 + CONV_ROWS, cols] = acc

    y = _conv_tail(c_ref[...], lng_ref[...], lnb_ref[...], w2_ref[...], b2_ref[...])
    o_ref[...] = x + y
    st_ref[...] = ubuf_ref[0, tt + lead:tt + CONV_CARRY, :]
    for s in range(SUBLANE):
        ubuf_ref[s, 0:CONV_CARRY, :] = ubuf_ref[s, tt:tt + CONV_CARRY, :]


def _conv_prompt(x, seq, gm, w1, b1, wdw, bdw, lng, lnb, w2, b2, *, tt):
    n, d = x.shape
    batch = n // seq
    nt = seq // tt
    assert seq % tt == 0 and tt % CONV_ROWS == 0 and d % CONV_COLS == 0
    row = lambda a: a.reshape(1, -1)
    return pl.pallas_call(
        _conv_prompt_kernel,
        grid=(batch, nt),
        in_specs=[
            pl.BlockSpec((tt, d), lambda b, t: (b * nt + t, 0)),
            _const_spec((1, d)), _const_spec((d, 2 * d)), _const_spec((1, 2 * d)),
            _const_spec((CONV_WIDTH, d)), _const_spec((1, d)), _const_spec((1, d)), _const_spec((1, d)),
            _const_spec((d, d)), _const_spec((1, d)),
        ],
        out_specs=[
            pl.BlockSpec((tt, d), lambda b, t: (b * nt + t, 0)),
            pl.BlockSpec((None, CONV_PREV, d), lambda b, t: (b, 0, 0)),
        ],
        out_shape=[jax.ShapeDtypeStruct((n, d), F32), jax.ShapeDtypeStruct((batch, CONV_PREV, d), F32)],
        scratch_shapes=[pltpu.VMEM((SUBLANE, tt + CONV_CARRY, d), F32), pltpu.VMEM((tt, d), F32)],
        compiler_params=_cparams(("parallel", "arbitrary"), 56),
        name="conv_prompt",
    )(x, row(gm), w1, row(b1), wdw, row(bdw), row(lng), row(lnb), w2, row(b2))


def _conv_sample_kernel(x_ref, st_ref, gm_ref, w1_ref, b1_ref, wdw_ref, bdw_ref, lng_ref, lnb_ref, w2_ref, b2_ref,
                        o_ref, ns_ref):
    nt, bb, d = x_ref.shape
    x = x_ref[...].reshape(nt * bb, d)
    u = _conv_glu(x, gm_ref[...], w1_ref[...], b1_ref[...])
    us = [u[i * bb:(i + 1) * bb] for i in range(nt)]

    def buf(j):
        return st_ref[j] if j < CONV_PREV else us[j - CONV_PREV]

    cs = []
    for t in range(nt):
        acc = jnp.broadcast_to(bdw_ref[...], (bb, d))
        for k in range(CONV_WIDTH):
            acc = acc + buf(t + k) * wdw_ref[k:k + 1, :]
        cs.append(acc)
    c = jnp.concatenate(cs, axis=0)
    y = _conv_tail(c, lng_ref[...], lnb_ref[...], w2_ref[...], b2_ref[...])
    o_ref[...] = (x + y).reshape(nt, bb, d)
    keep = CONV_PREV - nt
    ns_ref[0:keep] = st_ref[nt:CONV_PREV]
    for i in range(nt):
        ns_ref[keep + i] = us[i]


def _conv_sample(x3, states, layer, gm, w1, b1, wdw, bdw, lng, lnb, w2, b2, *, bb):
    nt, db, d = x3.shape
    assert db % bb == 0 and nt <= CONV_PREV
    row = lambda a: a.reshape(1, -1)
    return pl.pallas_call(
        _conv_sample_kernel,
        grid=(db // bb,),
        in_specs=[
            pl.BlockSpec((nt, bb, d), lambda i: (0, i, 0)),
            pl.BlockSpec((None, CONV_PREV, bb, d), lambda i: (layer, 0, i, 0)),
            _const_spec((1, d)), _const_spec((d, 2 * d)), _const_spec((1, 2 * d)),
            _const_spec((CONV_WIDTH, d)), _const_spec((1, d)), _const_spec((1, d)), _const_spec((1, d)),
            _const_spec((d, d)), _const_spec((1, d)),
        ],
        out_specs=[
            pl.BlockSpec((nt, bb, d), lambda i: (0, i, 0)),
            pl.BlockSpec((CONV_PREV, bb, d), lambda i: (0, i, 0)),
        ],
        out_shape=[jax.ShapeDtypeStruct((nt, db, d), F32), jax.ShapeDtypeStruct((CONV_PREV, db, d), F32)],
        compiler_params=_cparams(("parallel",), 48),
        name="conv_sample",
    )(x3, states, row(gm), w1, row(b1), wdw, row(bdw), row(lng), row(lnb), w2, row(b2))


def _mla_proj_kernel(x_ref, gm_ref, wdq_ref, gq_ref, wuq_ref, wdkv_ref, gkv_ref, wuk_ref, cos_ref, sin_ref,
                     *rest, transposed):
    if transposed:
        cos_t_ref, sin_t_ref, q_ref, k_ref, kt_ref, ckv_ref, kr_ref = rest
    else:
        q_ref, k_ref, ckv_ref, kr_ref = rest
    h = _rms(x_ref[...], gm_ref[...]).astype(BF16)
    cos = cos_ref[...]
    sin = sin_ref[...]
    cq = _rms(_mm(h, wdq_ref[...]), gq_ref[...])
    nh = q_ref.shape[0]
    if transposed:
        qt = _mm(wuq_ref[...], cq.T.astype(BF16))
        cos_t = cos_t_ref[...]
        sin_t = sin_t_ref[...]
        for i in range(nh):
            base = i * 3 * LANE
            q_lat = _mm(wuk_ref[i], qt[base:base + LANE].astype(BF16))
            q_rot = qt[base + LANE:base + 2 * LANE] * cos_t + qt[base + 2 * LANE:base + 3 * LANE] * sin_t
            q_ref[i, 0:MLA_KV_LORA, :] = (q_lat * MLA_QSCALE).astype(BF16)
            q_ref[i, MLA_KV_LORA:MLA_QK, :] = (q_rot * MLA_QSCALE).astype(BF16)
    else:
        q = _mm(cq.astype(BF16), wuq_ref[...])
        for i in range(nh):
            base = i * 3 * LANE
            q_lat = _mm(q[:, base:base + LANE].astype(BF16), wuk_ref[i])
            q_rot = q[:, base + LANE:base + 2 * LANE] * cos + q[:, base + 2 * LANE:base + 3 * LANE] * sin
            q_ref[i, :, 0:MLA_KV_LORA] = (q_lat * MLA_QSCALE).astype(BF16)
            q_ref[i, :, MLA_KV_LORA:MLA_QK] = (q_rot * MLA_QSCALE).astype(BF16)
    a = _mm(h, wdkv_ref[...])
    ckv = _rms(a[:, 0:MLA_KV_LORA], gkv_ref[...])
    k_rot = a[:, MLA_KV_LORA:MLA_QK] * cos + a[:, MLA_QK:MLA_QK + LANE] * sin
    ckv_ref[...] = ckv
    kr_ref[...] = k_rot[:, 0:MLA_ROPE]
    k_ref[:, 0:MLA_KV_LORA] = ckv.astype(BF16)
    k_ref[:, MLA_KV_LORA:MLA_QK] = k_rot.astype(BF16)
    if transposed:
        kt_ref[...] = jnp.concatenate([ckv, k_rot], axis=1).T.astype(BF16)


def _mla_proj(x, gm, w, cos, sin, *, tm, transposed):
    n, d = x.shape
    period = cos.shape[0] // tm
    assert n % tm == 0 and cos.shape[0] % tm == 0
    ql = w["dq"].shape[1]
    wuq, wuk = (w["uq_t"], w["uk_t"]) if transposed else (w["uq"], w["uk"])
    tok = lambda width: pl.BlockSpec((tm, width), lambda i: (i, 0))
    in_specs = [
        tok(d), _const_spec((1, d)), _const_spec((d, ql)), _const_spec((1, ql)),
        _const_spec(wuq.shape), _const_spec(w["dkv"].shape), _const_spec((1, MLA_KV_LORA)), _const_spec(wuk.shape),
        pl.BlockSpec((tm, LANE), lambda i: (i % period, 0)),
        pl.BlockSpec((tm, LANE), lambda i: (i % period, 0)),
    ]
    args = [x, gm.reshape(1, d), w["dq"], w["gq"].reshape(1, ql), wuq, w["dkv"], w["gkv"].reshape(1, MLA_KV_LORA),
            wuk, cos, sin]
    out_specs = [tok(MLA_QK), tok(MLA_KV_LORA), tok(MLA_ROPE)]
    out_shape = [jax.ShapeDtypeStruct((n, MLA_QK), BF16), jax.ShapeDtypeStruct((n, MLA_KV_LORA), F32),
                 jax.ShapeDtypeStruct((n, MLA_ROPE), F32)]
    if transposed:
        in_specs += [pl.BlockSpec((LANE, tm), lambda i: (0, i % period))] * 2
        args += [cos.T, sin.T]
        out_specs = ([pl.BlockSpec((MLA_HEADS, MLA_QK, tm), lambda i: (0, 0, i)), out_specs[0],
                      pl.BlockSpec((MLA_QK, tm), lambda i: (0, i))] + out_specs[1:])
        out_shape = ([jax.ShapeDtypeStruct((MLA_HEADS, MLA_QK, n), BF16), out_shape[0],
                      jax.ShapeDtypeStruct((MLA_QK, n), BF16)] + out_shape[1:])
    else:
        out_specs = [pl.BlockSpec((MLA_HEADS, tm, MLA_QK), lambda i: (0, i, 0))] + out_specs
        out_shape = [jax.ShapeDtypeStruct((MLA_HEADS, n, MLA_QK), BF16)] + out_shape
    return pl.pallas_call(
        functools.partial(_mla_proj_kernel, transposed=transposed),
        grid=(n // tm,),
        in_specs=in_specs,
        out_specs=out_specs,
        out_shape=out_shape,
        compiler_params=_cparams(("parallel",), 48),
        name="mla_proj_t" if transposed else "mla_proj",
    )(*args)


def _mla_attn_prompt_kernel(qi_ref, kj_ref, qt_ref, k_ref, vt_ref, o_ref, m_ref, l_ref, acc_ref):
    nh, dq, tq = qt_ref.shape
    tk = k_ref.shape[0]
    p = pl.program_id(1)
    i = qi_ref[p]
    j = kj_ref[p]

    @pl.when(j == 0)
    def _():
        m_ref[...] = jnp.full(m_ref.shape, NEG_INF, F32)
        l_ref[...] = jnp.zeros(l_ref.shape, F32)
        acc_ref[...] = jnp.zeros(acc_ref.shape, F32)

    def scores(h):
        return _mm(k_ref[...], qt_ref[h])

    def update(h, s, diagonal):
        if diagonal:
            causal = lax.broadcasted_iota(jnp.int32, (tk, tq), 0) <= lax.broadcasted_iota(jnp.int32, (tk, tq), 1)
            s = jnp.where(causal, s, NEG_INF)
        m_prev = m_ref[h]
        m_new = jnp.maximum(m_prev, jnp.max(s, axis=0, keepdims=True))
        alpha = jnp.exp2(m_prev - m_new)
        e = jnp.exp2(s - m_new)
        l_new = alpha * l_ref[h] + jnp.sum(e, axis=0, keepdims=True)
        acc = alpha * acc_ref[h] + _mm(vt_ref[...], e.astype(BF16))
        if diagonal:
            o_ref[h] = (acc / l_new).T.astype(o_ref.dtype)
        else:
            m_ref[h] = m_new
            l_ref[h] = l_new
            acc_ref[h] = acc

    def all_heads(diagonal):
        pending = [scores(h) for h in range(min(MLA_SCORE_AHEAD, nh))]
        for h in range(nh):
            if h + MLA_SCORE_AHEAD < nh:
                pending.append(scores(h + MLA_SCORE_AHEAD))
            update(h, pending.pop(0), diagonal)

    @pl.when(j < i)
    def _():
        all_heads(False)

    @pl.when(j == i)
    def _():
        all_heads(True)


def _mla_attn_prompt(qt, k, kt, seq, *, tq):
    nh, dq, n = qt.shape
    batch = n // seq
    nb = seq // tq
    assert seq % tq == 0
    pairs = [(i, j) for i in range(nb) for j in range(i + 1)]
    qi = jnp.asarray([p[0] for p in pairs], jnp.int32)
    kj = jnp.asarray([p[1] for p in pairs], jnp.int32)
    grid_spec = pltpu.PrefetchScalarGridSpec(
        num_scalar_prefetch=2,
        grid=(batch, len(pairs)),
        in_specs=[
            pl.BlockSpec((nh, dq, tq), lambda b, p, qi, kj: (0, 0, b * nb + qi[p])),
            pl.BlockSpec((tq, dq), lambda b, p, qi, kj: (b * nb + kj[p], 0)),
            pl.BlockSpec((MLA_KV_LORA, tq), lambda b, p, qi, kj: (0, b * nb + kj[p])),
        ],
        out_specs=pl.BlockSpec((nh, tq, MLA_KV_LORA), lambda b, p, qi, kj: (0, b * nb + qi[p], 0)),
        scratch_shapes=[pltpu.VMEM((nh, 1, tq), F32), pltpu.VMEM((nh, 1, tq), F32),
                        pltpu.VMEM((nh, MLA_KV_LORA, tq), F32)],
    )
    return pl.pallas_call(
        _mla_attn_prompt_kernel,
        grid_spec=grid_spec,
        out_shape=jax.ShapeDtypeStruct((nh, n, MLA_KV_LORA), BF16),
        compiler_params=_cparams(("parallel", "arbitrary"), 48),
        name="mla_attn_prompt",
    )(qi, kj, qt, k, kt)


MLA_SCORE_AHEAD = 2
MLA_SAMPLE_CHUNK = 1024
MLA_NEW_PAD = 128
MLA_PAGE_UNROLL = 8


def _mla_attn_sample_kernel(pt_ref, q_ref, kn_ref, lat_hbm, kr_hbm, o_ref, lat_buf, kr_buf, lat_bf, s_ref, sem, *,
                            n_new):
    b = pl.program_id(0)
    nb = pl.num_programs(0)
    past = lat_buf.shape[1]
    n_pages = past // PAGE_SIZE
    rows = q_ref.shape[0]
    slot = b % 2

    def page_copies(bi, sl, pg):
        page = pt_ref[bi * n_pages + pg]
        dst = pl.ds(pg * PAGE_SIZE, PAGE_SIZE)
        return (pltpu.make_async_copy(lat_hbm.at[page], lat_buf.at[sl, dst], sem.at[0, sl]),
                pltpu.make_async_copy(kr_hbm.at[page], kr_buf.at[sl, :, dst], sem.at[1, sl]))

    def for_each_page(bi, sl, act):
        def body(g, _):
            for u in range(MLA_PAGE_UNROLL):
                for c in page_copies(bi, sl, g * MLA_PAGE_UNROLL + u):
                    act(c)
            return 0
        lax.fori_loop(0, n_pages // MLA_PAGE_UNROLL, body, 0)

    @pl.when(b == 0)
    def _():
        for_each_page(0, 0, lambda c: c.start())

    for_each_page(b, slot, lambda c: c.wait())

    for pg in range(n_pages):
        for c in page_copies(b + 1, 1 - slot, pg):
            c.start()

    q = q_ref[...]
    q_lat = q[:, 0:MLA_KV_LORA]
    q_rope = q[:, MLA_KV_LORA:MLA_KV_LORA + MLA_ROPE]
    ch = MLA_SAMPLE_CHUNK
    n_chunks = past // ch

    m = jnp.full((rows, 1), NEG_INF, F32)
    for c in range(n_chunks):
        keys = slice(c * ch, (c + 1) * ch)
        lat = lat_buf[slot, keys, :].astype(BF16)
        lat_bf[keys, :] = lat
        s = _mm_nt(q_lat, lat) + _mm(q_rope, kr_buf[slot, :, keys].astype(BF16))
        s_ref[:, keys] = s
        m = jnp.maximum(m, jnp.max(s, axis=-1, keepdims=True))

    kn = kn_ref[...]
    k_new = jnp.concatenate([kn, jnp.zeros((MLA_NEW_PAD - kn.shape[0], kn.shape[1]), F32)], axis=0).astype(BF16)
    s_new = _mm_nt(q, k_new)
    r_tok = lax.broadcasted_iota(jnp.int32, (rows, MLA_NEW_PAD), 0) % n_new
    s_new = jnp.where(lax.broadcasted_iota(jnp.int32, (rows, MLA_NEW_PAD), 1) <= r_tok, s_new, NEG_INF)
    m = jnp.maximum(m, jnp.max(s_new, axis=-1, keepdims=True))
    e_new = jnp.exp2(s_new - m)
    l0 = jnp.sum(e_new, axis=-1, keepdims=True)
    acc0 = _mm(e_new.astype(BF16), k_new[:, 0:MLA_KV_LORA])

    l, acc = l0, acc0
    for c in range(n_chunks):
        keys = slice(c * ch, (c + 1) * ch)
        e = jnp.exp2(s_ref[:, keys] - m)
        l = l + jnp.sum(e, axis=-1, keepdims=True)
        acc = acc + _mm(e.astype(BF16), lat_bf[keys, :])
    o_ref[...] = (acc / l).astype(o_ref.dtype)

    @pl.when(b == nb - 1)
    def _():
        for_each_page(b + 1, 1 - slot, lambda c: c.wait())


def _mla_attn_sample(q, k_new, cache_lat, cache_kr, page_table, n_new):
    db, rows, dq = q.shape
    tp = k_new.shape[1]
    n_pages = page_table.shape[1]
    past = n_pages * PAGE_SIZE
    assert past % MLA_SAMPLE_CHUNK == 0 and n_pages % MLA_PAGE_UNROLL == 0
    grid_spec = pltpu.PrefetchScalarGridSpec(
        num_scalar_prefetch=1,
        grid=(db,),
        in_specs=[
            pl.BlockSpec((None, rows, dq), lambda b, pt: (b, 0, 0)),
            pl.BlockSpec((None, tp, dq), lambda b, pt: (b, 0, 0)),
            pl.BlockSpec(memory_space=pl.ANY),
            pl.BlockSpec(memory_space=pl.ANY),
        ],
        out_specs=pl.BlockSpec((None, rows, MLA_KV_LORA), lambda b, pt: (b, 0, 0)),
        scratch_shapes=[
            pltpu.VMEM((2, past, MLA_KV_LORA), F32),
            pltpu.VMEM((2, MLA_ROPE, past), F32),
            pltpu.VMEM((past, MLA_KV_LORA), BF16),
            pltpu.VMEM((rows, past), F32),
            pltpu.SemaphoreType.DMA((2, 2)),
        ],
    )
    return pl.pallas_call(
        functools.partial(_mla_attn_sample_kernel, n_new=n_new),
        grid_spec=grid_spec,
        out_shape=jax.ShapeDtypeStruct((db, rows, MLA_KV_LORA), BF16),
        compiler_params=_cparams(("arbitrary",), 48),
        name="mla_attn_sample",
    )(jnp.concatenate([page_table, page_table[:1]], axis=0).reshape(-1), q, k_new, cache_lat, cache_kr)


def _mla_out_kernel(x_ref, a_ref, wuv_ref, wo_ref, o_ref):
    nh = a_ref.shape[0]
    o = jnp.concatenate([_mm(a_ref[i], wuv_ref[i]).astype(BF16) for i in range(nh)], axis=-1)
    o_ref[...] = x_ref[...] + _mm(o, wo_ref[...])


def _mla_out(x, a, wuv, wo, *, tm):
    n, d = x.shape
    assert n % tm == 0
    return pl.pallas_call(
        _mla_out_kernel,
        grid=(n // tm,),
        in_specs=[
            pl.BlockSpec((tm, d), lambda i: (i, 0)),
            pl.BlockSpec((MLA_HEADS, tm, MLA_KV_LORA), lambda i: (0, i, 0)),
            _const_spec(wuv.shape), _const_spec(wo.shape),
        ],
        out_specs=pl.BlockSpec((tm, d), lambda i: (i, 0)),
        out_shape=jax.ShapeDtypeStruct((n, d), F32),
        compiler_params=_cparams(("parallel",), 48),
        name="mla_out",
    )(x, a, wuv, wo)


def _ret_project(x, gm, wq, wk, wv, wg, cos, sin, n_heads):
    h = _rms(x, gm).astype(BF16)
    half = RET_DK // 2

    def rot(a):
        outs = []
        for i in range(n_heads):
            a1 = a[:, i * RET_DK:i * RET_DK + half]
            a2 = a[:, i * RET_DK + half:(i + 1) * RET_DK]
            outs += [a1 * cos - a2 * sin, a2 * cos + a1 * sin]
        return jnp.concatenate(outs, axis=-1)

    return rot(_mm(h, wq)), rot(_mm(h, wk)) * (RET_DK ** -0.5), _mm(h, wv), _mm(h, wg)


def _ret_gate(o, g, gng, gnb):
    mu = jnp.mean(o, axis=-1, keepdims=True)
    oc = o - mu
    on = oc * lax.rsqrt(jnp.mean(oc * oc, axis=-1, keepdims=True) + EPS)
    return (_silu(g) * (on * gng + gnb)).astype(BF16)


def _ret_prompt_kernel(x_ref, gm_ref, wq_ref, wk_ref, wv_ref, wg_ref, cos_ref, sin_ref,
                       decay_ref, cross_ref, kdec_ref, gl_ref, gng_ref, gnb_ref, wo_ref,
                       o_ref, st_ref, state_ref, y_ref):
    tt, d = x_ref.shape
    nh, lc, _ = decay_ref.shape
    c = pl.program_id(1)

    @pl.when(c == 0)
    def _():
        state_ref[...] = jnp.zeros(state_ref.shape, F32)

    x = x_ref[...]
    q, k, v, g = _ret_project(x, gm_ref[...], wq_ref[...], wk_ref[...], wv_ref[...], wg_ref[...],
                              cos_ref[...], sin_ref[...], nh)
    y_ref[...] = x

    def retain(r0, i):
        rows = slice(r0, r0 + lc)
        qh = q[rows, i * RET_DK:(i + 1) * RET_DK].astype(BF16)
        kh = k[rows, i * RET_DK:(i + 1) * RET_DK]
        vh = v[rows, i * RET_DV:(i + 1) * RET_DV].astype(BF16)
        state = state_ref[i]
        inner = _mm_nt(qh, kh.astype(BF16)) * decay_ref[i]
        o = _mm(inner.astype(BF16), vh) + _mm(qh, state.astype(BF16)) * cross_ref[i]
        k_dec = (kh * kdec_ref[i]).T.astype(BF16)
        state_ref[i] = state * gl_ref[i] + _mm(k_dec, vh)
        return o

    def emit(r0, i, o):
        rows = slice(r0, r0 + lc)
        hv = slice(i * RET_DV, (i + 1) * RET_DV)
        gated = _ret_gate(o, g[rows, hv], gng_ref[:, hv], gnb_ref[:, hv])
        y_ref[rows, :] += _mm(gated, wo_ref[hv, :])

    steps = [(r0, i) for r0 in range(0, tt, lc) for i in range(nh)]
    o_next = retain(*steps[0])
    for n, step in enumerate(steps):
        o = o_next
        if n + 1 < len(steps):
            o_next = retain(*steps[n + 1])
        emit(*step, o)
    o_ref[...] = y_ref[...]

    @pl.when(c == pl.num_programs(1) - 1)
    def _():
        st_ref[...] = state_ref[...]


def _ret_tables(n_heads, lc):
    log_gamma = jnp.log(1.0 - 2.0 ** (-5.0 - jnp.arange(n_heads, dtype=F32)))
    n = jnp.arange(lc, dtype=F32)
    diff = n[:, None] - n[None, :]
    decay = jnp.exp(jnp.where(diff[None] >= 0, diff[None] * log_gamma[:, None, None], -jnp.inf))
    cross = jnp.exp((n[None, :] + 1.0) * log_gamma[:, None])[:, :, None]
    kdec = jnp.exp((lc - 1.0 - n)[None, :] * log_gamma[:, None])[:, :, None]
    gl = jnp.exp(lc * log_gamma)
    return decay, cross, kdec, gl


def _ret_prompt(x, seq, gm, w, cos, sin, *, tt, lc):
    n, d = x.shape
    batch = n // seq
    nt = seq // tt
    nh = w["q"].shape[1] // RET_DK
    assert seq % tt == 0 and tt % lc == 0
    decay, cross, kdec, gl = _ret_tables(nh, lc)
    gl = jnp.broadcast_to(gl[:, None, None], (nh, 1, RET_DV))
    dv = nh * RET_DV
    return pl.pallas_call(
        _ret_prompt_kernel,
        grid=(batch, nt),
        in_specs=[
            pl.BlockSpec((tt, d), lambda b, c: (b * nt + c, 0)),
            _const_spec((1, d)),
            _const_spec(w["q"].shape), _const_spec(w["k"].shape), _const_spec(w["v"].shape),
            _const_spec(w["g"].shape),
            pl.BlockSpec((tt, RET_DK // 2), lambda b, c: (c, 0)),
            pl.BlockSpec((tt, RET_DK // 2), lambda b, c: (c, 0)),
            _const_spec(decay.shape), _const_spec(cross.shape), _const_spec(kdec.shape), _const_spec(gl.shape),
            _const_spec((1, dv)), _const_spec((1, dv)), _const_spec(w["o"].shape),
        ],
        out_specs=[
            pl.BlockSpec((tt, d), lambda b, c: (b * nt + c, 0)),
            pl.BlockSpec((None, nh, RET_DK, RET_DV), lambda b, c: (b, 0, 0, 0)),
        ],
        out_shape=[jax.ShapeDtypeStruct((n, d), F32), jax.ShapeDtypeStruct((batch, nh, RET_DK, RET_DV), F32)],
        scratch_shapes=[pltpu.VMEM((nh, RET_DK, RET_DV), F32), pltpu.VMEM((tt, d), F32)],
        compiler_params=_cparams(("parallel", "arbitrary"), 56),
        name="ret_prompt",
    )(x, gm.reshape(1, d), w["q"], w["k"], w["v"], w["g"], cos, sin, decay, cross, kdec, gl,
      w["gng"].reshape(1, dv), w["gnb"].reshape(1, dv), w["o"])


def _ret_proj_kernel(x_ref, gm_ref, wq_ref, wk_ref, wv_ref, wg_ref, cos_ref, sin_ref, q_ref, k_ref, v_ref, g_ref):
    nh = wq_ref.shape[1] // RET_DK
    q, k, v, g = _ret_project(x_ref[...], gm_ref[...], wq_ref[...], wk_ref[...], wv_ref[...], wg_ref[...],
                              cos_ref[...], sin_ref[...], nh)
    q_ref[...] = q
    k_ref[...] = k
    v_ref[...] = v
    g_ref[...] = g


def _ret_proj(x, gm, w, cos, sin, *, tm):
    n, d = x.shape
    dk = w["q"].shape[1]
    dv = w["v"].shape[1]
    assert n % tm == 0
    tok = lambda width: pl.BlockSpec((tm, width), lambda i: (i, 0))
    return pl.pallas_call(
        _ret_proj_kernel,
        grid=(n // tm,),
        in_specs=[tok(d), _const_spec((1, d)), _const_spec(w["q"].shape), _const_spec(w["k"].shape),
                  _const_spec(w["v"].shape), _const_spec(w["g"].shape), tok(RET_DK // 2), tok(RET_DK // 2)],
        out_specs=[tok(dk), tok(dk), tok(dv), tok(dv)],
        out_shape=[jax.ShapeDtypeStruct((n, dk), F32), jax.ShapeDtypeStruct((n, dk), F32),
                   jax.ShapeDtypeStruct((n, dv), F32), jax.ShapeDtypeStruct((n, dv), F32)],
        compiler_params=_cparams(("parallel",), 48),
        name="ret_proj",
    )(x, gm.reshape(1, d), w["q"], w["k"], w["v"], w["g"], cos, sin)


RET_NEW_PAD = 128
RET_SAMPLE_BLOCK = 2


def _ret_sample_kernel(q_ref, k_ref, v_ref, st_ref, decay_ref, cross_ref, kdec_ref, gl_ref, o_ref, ns_ref):
    bb, tp, _ = q_ref.shape
    nh = st_ref.shape[1]
    zpad = RET_NEW_PAD - tp
    for b in range(bb):
        for i in range(nh):
            qh = q_ref[b, :, i * RET_DK:(i + 1) * RET_DK].astype(BF16)
            kh = jnp.concatenate([k_ref[b, :, i * RET_DK:(i + 1) * RET_DK], jnp.zeros((zpad, RET_DK), F32)], axis=0)
            vh = jnp.concatenate([v_ref[b, :, i * RET_DV:(i + 1) * RET_DV], jnp.zeros((zpad, RET_DV), F32)],
                                 axis=0).astype(BF16)
            kt = kh.T
            state = st_ref[b, i]
            inner = _mm(qh, kt.astype(BF16)) * decay_ref[i]
            o = _mm(inner.astype(BF16), vh) + _mm(qh, state.astype(BF16)) * cross_ref[i]
            o_ref[b, :, i * RET_DV:(i + 1) * RET_DV] = o
            ns_ref[b, i] = state * gl_ref[i] + _mm((kt * kdec_ref[i]).astype(BF16), vh)


def _ret_sample(q, k, v, state, n_new):
    db, tp, dk = q.shape
    dv = v.shape[2]
    nh = state.shape[1]
    log_gamma = jnp.log(1.0 - 2.0 ** (-5.0 - jnp.arange(nh, dtype=F32)))
    n = jnp.arange(tp, dtype=F32)
    m = jnp.arange(RET_NEW_PAD, dtype=F32)
    diff = n[:, None] - m[None, :]
    live = (diff >= 0) & (m[None, :] < n_new)
    decay = jnp.exp(jnp.where(live[None], diff[None] * log_gamma[:, None, None], -jnp.inf))
    cross = jnp.exp((n[None, :] + 1.0) * log_gamma[:, None])[:, :, None]
    kdec = jnp.where(m[None, :] < n_new, jnp.exp((n_new - 1.0 - m)[None, :] * log_gamma[:, None]), 0.0)[:, None, :]
    gl = jnp.broadcast_to(jnp.exp(n_new * log_gamma)[:, None, None], (nh, 1, RET_DV))
    bb = RET_SAMPLE_BLOCK if db % RET_SAMPLE_BLOCK == 0 else 1
    return pl.pallas_call(
        _ret_sample_kernel,
        grid=(db // bb,),
        in_specs=[
            pl.BlockSpec((bb, tp, dk), lambda b: (b, 0, 0)),
            pl.BlockSpec((bb, tp, dk), lambda b: (b, 0, 0)),
            pl.BlockSpec((bb, tp, dv), lambda b: (b, 0, 0)),
            pl.BlockSpec((bb, nh, RET_DK, RET_DV), lambda b: (b, 0, 0, 0)),
            _const_spec(decay.shape), _const_spec(cross.shape), _const_spec(kdec.shape), _const_spec(gl.shape),
        ],
        out_specs=[
            pl.BlockSpec((bb, tp, dv), lambda b: (b, 0, 0)),
            pl.BlockSpec((bb, nh, RET_DK, RET_DV), lambda b: (b, 0, 0, 0)),
        ],
        out_shape=[jax.ShapeDtypeStruct((db, tp, dv), F32), jax.ShapeDtypeStruct(state.shape, F32)],
        compiler_params=_cparams(("parallel",), 40),
        name="ret_sample",
    )(q, k, v, state, decay, cross, kdec, gl)


def _ret_out_kernel(x_ref, o_ref_in, g_ref, gng_ref, gnb_ref, wo_ref, o_ref):
    nh = o_ref_in.shape[1] // RET_DV
    gated = jnp.concatenate(
        [_ret_gate(o_ref_in[:, i * RET_DV:(i + 1) * RET_DV], g_ref[:, i * RET_DV:(i + 1) * RET_DV],
                   gng_ref[:, i * RET_DV:(i + 1) * RET_DV], gnb_ref[:, i * RET_DV:(i + 1) * RET_DV])
         for i in range(nh)], axis=-1)
    o_ref[...] = x_ref[...] + _mm(gated, wo_ref[...])


def _ret_out(x, o, g, gng, gnb, wo, *, tm):
    n, d = x.shape
    dv = o.shape[1]
    assert n % tm == 0
    tok = lambda width: pl.BlockSpec((tm, width), lambda i: (i, 0))
    return pl.pallas_call(
        _ret_out_kernel,
        grid=(n // tm,),
        in_specs=[tok(d), tok(dv), tok(dv), _const_spec((1, dv)), _const_spec((1, dv)), _const_spec(wo.shape)],
        out_specs=tok(d),
        out_shape=jax.ShapeDtypeStruct((n, d), F32),
        compiler_params=_cparams(("parallel",), 48),
        name="ret_out",
    )(x, o, g, gng.reshape(1, dv), gnb.reshape(1, dv), wo)


def _rope_tables(pos, dim):
    if dim == MLA_ROPE:
        inv = ROPE_BASE ** (-jnp.arange(0, dim, 2, dtype=F32) / dim)
    else:
        inv = 1.0 / (ROPE_BASE ** jnp.linspace(0.0, 1.0, dim // 2, dtype=F32))
    ang = pos.astype(F32)[:, None] * inv[None, :]
    return jnp.cos(ang), jnp.sin(ang)


def _mla_tables(pos):
    cos, sin = _rope_tables(pos, MLA_ROPE)
    z = jnp.zeros((pos.shape[0], LANE - MLA_ROPE), F32)
    return jnp.concatenate([cos, cos, z], axis=1), jnp.concatenate([sin, sin, z], axis=1)


def _rot_half_cols(w):
    half = w.shape[-1] // 2
    return jnp.concatenate([-w[..., half:], w[..., :half]], axis=-1)


def _pad_cols(w, width):
    return jnp.pad(w, [(0, 0)] * (w.ndim - 1) + [(0, width - w.shape[-1])])


def _mla_weights(w_dq, g_q, w_uq, w_dkv, g_kv, w_uk, w_uv, w_o):
    ql = w_dq.shape[1]
    uq = w_uq.reshape(ql, MLA_HEADS, MLA_NOPE + MLA_ROPE)
    rope = uq[..., MLA_NOPE:]
    uq = jnp.concatenate([uq[..., :MLA_NOPE], _pad_cols(rope, LANE), _pad_cols(_rot_half_cols(rope), LANE)], axis=-1)
    k_rope = w_dkv[:, MLA_KV_LORA:]
    dkv = jnp.concatenate([w_dkv[:, :MLA_KV_LORA], _pad_cols(k_rope, LANE), _pad_cols(_rot_half_cols(k_rope), LANE)],
                          axis=-1)
    uq = uq.reshape(ql, MLA_HEADS * 3 * LANE).astype(BF16)
    return {
        "dq": w_dq.astype(BF16), "gq": g_q, "uq": uq, "uq_t": uq.T,
        "dkv": dkv.astype(BF16), "gkv": g_kv,
        "uk": jnp.transpose(w_uk, (1, 2, 0)).astype(BF16),
        "uk_t": jnp.transpose(w_uk, (1, 0, 2)).astype(BF16),
        "uv": jnp.transpose(w_uv, (1, 0, 2)).astype(BF16),
        "o": w_o.astype(BF16),
    }


def kernel(x_prompt, x_sample, state_conv, cache_mla_latent, cache_mla_krope, state_ret, page_table, norm_mix, norm_ffn, norm_final, conv_w_pw1, conv_b_pw1, conv_w_dw, conv_b_dw, conv_ln_g, conv_ln_b, conv_w_pw2, conv_b_pw2, mla_w_dq, mla_g_q, mla_w_uq, mla_w_dkv, mla_g_kv, mla_w_uk, mla_w_uv, mla_w_o, ret_w_q, ret_w_k, ret_w_v, ret_w_g, ret_gn_g, ret_gn_b, ret_w_o, ffn_w1, ffn_w3, ffn_w2):
    batch, seq, d = x_prompt.shape
    db, n_new, _ = x_sample.shape
    depth = norm_mix.shape[0]
    past = page_table.shape[1] * PAGE_SIZE
    ns = db * n_new
    tm_p = min(512, seq)
    tm_s = min(256, ns)

    xp = x_prompt.reshape(batch * seq, d)
    xs = jnp.transpose(x_sample, (1, 0, 2)).reshape(ns, d)
    pos_p = jnp.arange(seq)
    pos_s = jnp.repeat(past + jnp.arange(n_new), db)

    def to_t_major(a):
        return jnp.swapaxes(a, 0, 1).reshape(ns, *a.shape[2:])

    def to_b_major(a):
        return jnp.swapaxes(a.reshape(n_new, db, *a.shape[1:]), 0, 1)

    ffn_w1_bf, ffn_w3_bf, ffn_w2_bf = ffn_w1.astype(BF16), ffn_w3.astype(BF16), ffn_w2.astype(BF16)
    conv_p, conv_s, lat_p, kr_p, lat_s, kr_s, ret_p, ret_s = [], [], [], [], [], [], [], []
    for i in range(depth):
        j = i // N_MIXERS
        kind = i % N_MIXERS
        if kind == 0:
            cw = (norm_mix[i], conv_w_pw1[j].astype(BF16), conv_b_pw1[j], conv_w_dw[j], conv_b_dw[j],
                  conv_ln_g[j], conv_ln_b[j], conv_w_pw2[j].astype(BF16), conv_b_pw2[j])
            xp, st_p = _conv_prompt(xp, seq, *cw, tt=tm_p)
            xs3, st_s = _conv_sample(xs.reshape(n_new, db, d), jnp.swapaxes(state_conv, 1, 2), j, *cw,
                                     bb=min(32, db))
            xs = xs3.reshape(ns, d)
            conv_p.append(st_p)
            conv_s.append(jnp.swapaxes(st_s, 0, 1))
        elif kind == 1:
            w = _mla_weights(mla_w_dq[j], mla_g_q[j], mla_w_uq[j], mla_w_dkv[j], mla_g_kv[j], mla_w_uk[j],
                             mla_w_uv[j], mla_w_o[j])
            q, k, kt, ckv, kr = _mla_proj(xp, norm_mix[i], w, *_mla_tables(pos_p), tm=tm_p, transposed=True)
            a = _mla_attn_prompt(q, k, kt, seq, tq=min(512, seq))
            xp = _mla_out(xp, a, w["uv"], w["o"], tm=tm_p)
            lat_p.append(ckv.reshape(batch, seq, MLA_KV_LORA))
            kr_p.append(kr.reshape(batch, seq, MLA_ROPE))

            q, k, ckv, kr = _mla_proj(xs, norm_mix[i], w, *_mla_tables(pos_s), tm=tm_s, transposed=False)
            q_b = jnp.transpose(q.reshape(MLA_HEADS, n_new, db, MLA_QK), (2, 0, 1, 3)).reshape(
                db, MLA_HEADS * n_new, MLA_QK)
            k_b = jnp.pad(to_b_major(k).astype(F32), ((0, 0), (0, 8 - n_new), (0, 0)))
            a_b = _mla_attn_sample(q_b, k_b, cache_mla_latent[j], jnp.swapaxes(cache_mla_krope[j], 1, 2),
                                   page_table, n_new)
            a = jnp.transpose(a_b.reshape(db, MLA_HEADS, n_new, MLA_KV_LORA), (1, 2, 0, 3)).reshape(
                MLA_HEADS, ns, MLA_KV_LORA)
            xs = _mla_out(xs, a, w["uv"], w["o"], tm=tm_s)
            lat_s.append(to_b_major(ckv))
            kr_s.append(to_b_major(kr))
        else:
            w = {"q": ret_w_q[j].astype(BF16), "k": ret_w_k[j].astype(BF16), "v": ret_w_v[j].astype(BF16),
                 "g": ret_w_g[j].astype(BF16), "gng": ret_gn_g[j], "gnb": ret_gn_b[j],
                 "o": ret_w_o[j].astype(BF16)}
            xp, st_p = _ret_prompt(xp, seq, norm_mix[i], w, *_rope_tables(pos_p, RET_DK), tt=tm_p,
                                   lc=min(256, seq))
            ret_p.append(st_p)

            q, k, v, g = _ret_proj(xs, norm_mix[i], w, *_rope_tables(pos_s, RET_DK), tm=tm_s)
            pad = lambda a: jnp.pad(to_b_major(a), ((0, 0), (0, 8 - n_new), (0, 0)))
            o_b, st_s = _ret_sample(pad(q), pad(k), pad(v), state_ret[j], n_new)
            xs = _ret_out(xs, to_t_major(o_b[:, :n_new]), g, w["gng"], w["gnb"], w["o"], tm=tm_s)
            ret_s.append(st_s)
        last = i == depth - 1
        fw = (norm_ffn[i], ffn_w1_bf, ffn_w3_bf, ffn_w2_bf, i, norm_final if last else None)
        xp = _ffn(xp, *fw, tm=tm_p)
        xs = _ffn(xs, *fw, tm=tm_s)

    return (xp.reshape(batch, seq, d), to_b_major(xs),
            jnp.stack(conv_p), jnp.stack(conv_s),
            jnp.stack(lat_p), jnp.stack(kr_p), jnp.stack(lat_s), jnp.stack(kr_s),
            jnp.stack(ret_p), jnp.stack(ret_s))
```
